```python
import math
import jax, jax.numpy as jnp
from jax import lax
import numpy as np

D_MODEL = 1024
BATCH = 8
SEQ = 2048
DEPTH = 2

CHUNK = 64
QBLOCK = 64
N_EVEN = (DEPTH + 1) // 2
N_ODD = DEPTH // 2

A_HEADS = 8
A_HEAD_DIM = 64
A_WIDTH = A_HEADS * A_HEAD_DIM
IDX_HEADS = 16
IDX_DIM = 64
TOPK_MAX = 256
REL_BUCKETS = 32
REL_MAX_DIST = 128
B_GROUPS = 8
B_GROUP_DIM = 64
B_WIDTH = B_GROUPS * B_GROUP_DIM
B_CHUNK = 128
C_HEADS = 8
C_KEY_DIM = 128
C_VAL_DIM = 128
C_WIDTH = C_HEADS * C_VAL_DIM
D_FF = 2816
CONV_WIDTH = 3
EPS = 1e-6

AB_SPLITS = (A_WIDTH, A_WIDTH, A_WIDTH, IDX_HEADS * IDX_DIM, IDX_DIM, IDX_HEADS, B_WIDTH, B_WIDTH)
AB_IN = sum(AB_SPLITS)
C_SPLITS = (C_HEADS * C_KEY_DIM, C_HEADS * C_KEY_DIM, C_WIDTH, C_WIDTH)
C_IN = sum(C_SPLITS)

kernel_name = 'hybrid_dsa_gmlp_hgrn2_convffn'

F32 = jnp.float32


def rms_norm(x, g):
    xf = x.astype(F32)
    y = xf * lax.rsqrt(jnp.mean(xf * xf, axis=-1, keepdims=True) + EPS)
    return (y * g.astype(F32)).astype(x.dtype)


def split_cols(h, sizes):
    offs = np.cumsum(sizes)[:-1].tolist()
    return jnp.split(h, offs, axis=-1)


def t5_bucket(rel):
    half = REL_BUCKETS // 2
    max_exact = half // 2
    ret = jnp.where(rel > 0, half, 0)
    n = jnp.abs(rel)
    nf = jnp.maximum(n, max_exact).astype(F32)
    large = max_exact + (jnp.log(nf / max_exact) / math.log(REL_MAX_DIST / max_exact)
                         * (half - max_exact)).astype(jnp.int32)
    large = jnp.minimum(large, half - 1)
    return ret + jnp.where(n < max_exact, n, large)


def dsa_attention(q, k, v, iq, ik, iw, rel_bias):
    bsz, seq = q.shape[0], q.shape[1]
    top_k = min(TOPK_MAX, seq // 4)
    n_blocks = seq // QBLOCK
    key_chunk = jnp.arange(seq) // CHUNK
    kv = jnp.concatenate([k, v], axis=-1)
    ik32 = ik.astype(F32)
    idx_scale = IDX_DIM ** -0.5
    w_scale = IDX_HEADS ** -0.5
    att_scale = A_HEAD_DIM ** -0.5

    def block(i):
        start = i * QBLOCK
        t = start + jnp.arange(QBLOCK)
        q_b = lax.dynamic_slice_in_dim(q, start, QBLOCK, axis=1)
        iq_b = lax.dynamic_slice_in_dim(iq, start, QBLOCK, axis=1)
        iw_b = lax.dynamic_slice_in_dim(iw, start, QBLOCK, axis=1)
        s = jnp.einsum('bqhd,bsd->bqhs', iq_b.astype(F32), ik32) * idx_scale
        score = jnp.einsum('bqhs,bqh->bqs', jax.nn.relu(s), iw_b.astype(F32) * w_scale)
        allowed = key_chunk[None, :] <= (t // CHUNK)[:, None]
        score = jnp.where(allowed[None], score, -jnp.inf)
        _, sel = lax.top_k(score, top_k)
        kv_sel = jax.vmap(lambda kvb, ib: kvb[ib])(kv, sel)
        k_sel, v_sel = jnp.split(kv_sel, 2, axis=-1)
        valid = (sel // CHUNK) <= (t // CHUNK)[None, :, None]
        bias = rel_bias[t5_bucket(sel - t[None, :, None])]
        logits = (jnp.einsum('bqhd,bqkhd->bhqk', q_b, k_sel).astype(F32) * att_scale
                  + jnp.transpose(bias, (0, 3, 1, 2)).astype(F32))
        logits = jnp.where(valid[:, None], logits, -jnp.inf)
        p = jax.nn.softmax(logits, axis=-1).astype(v.dtype)
        return jnp.einsum('bhqk,bqkhd->bqhd', p, v_sel)

    out = lax.map(block, jnp.arange(n_blocks))
    return jnp.transpose(out, (1, 0, 2, 3, 4)).reshape(bsz, seq, A_WIDTH)


def gmlp_spatial_gate(u, v, norm_g, w_s, b_s):
    bsz, seq = u.shape[0], u.shape[1]
    v = rms_norm(v, norm_g)
    pos_chunk = jnp.arange(B_CHUNK) // CHUNK
    mask = (pos_chunk[:, None] >= pos_chunk[None, :]).astype(w_s.dtype)
    vb = v.reshape(bsz, seq // B_CHUNK, B_CHUNK, B_GROUPS, B_GROUP_DIM)
    s = (jnp.einsum('gij,bnjgc->bnigc', w_s * mask, vb)
         + jnp.transpose(b_s)[None, None, :, :, None])
    return u * s.reshape(bsz, seq, B_WIDTH)


def hgrn2(q, f, i, lb):
    bsz, seq = q.shape[0], q.shape[1]
    lb = lb.astype(F32)
    q = jax.nn.silu(q.astype(F32))
    g = lb + (1.0 - lb) * jax.nn.sigmoid(f.astype(F32))
    k = 1.0 - g
    log_g = jnp.log(g)
    nc = seq // CHUNK

    def to_chunks(a, d):
        return a.reshape(bsz, nc, CHUNK, C_HEADS, d).transpose(1, 0, 3, 2, 4)

    qc = to_chunks(q, C_KEY_DIM)
    kc = to_chunks(k, C_KEY_DIM)
    lgc = to_chunks(log_g, C_KEY_DIM)
    ic = to_chunks(i.astype(F32), C_VAL_DIM)
    causal = (jnp.arange(CHUNK)[:, None] >= jnp.arange(CHUNK)[None, :])[:, :, None]

    def step(S, xs):
        qb, kb, ib, lgb = xs
        b = jnp.cumsum(lgb, axis=-2)
        inter = jnp.einsum('bhtd,bhde->bhte', qb * jnp.exp(b), S)
        diff = b[:, :, :, None, :] - b[:, :, None, :, :]
        decay = jnp.where(causal, jnp.exp(jnp.where(causal, diff, 0.0)), 0.0)
        a = jnp.einsum('bhtd,bhtsd,bhsd->bhts', qb, decay, kb)
        intra = jnp.einsum('bhts,bhse->bhte', a, ib)
        b_last = b[:, :, -1:, :]
        S_new = (jnp.exp(b_last[:, :, 0, :])[..., None] * S
                 + jnp.einsum('bhsd,bhse->bhde', kb * jnp.exp(b_last - b), ib))
        return S_new, inter + intra

    S0 = jnp.zeros((bsz, C_HEADS, C_KEY_DIM, C_VAL_DIM), F32)
    _, o = lax.scan(step, S0, (qc, kc, ic, lgc))
    return o.transpose(1, 0, 3, 2, 4).reshape(bsz, seq, C_HEADS, C_VAL_DIM)


def conv_ffn(x, w_up, conv_w, conv_b, w_down):
    seq = x.shape[1]
    h = x @ w_up
    hp = jnp.pad(h, ((0, 0), (CONV_WIDTH - 1, 0), (0, 0)))
    y = conv_b + conv_w[CONV_WIDTH - 1] * hp[:, CONV_WIDTH - 1:CONV_WIDTH - 1 + seq]
    for j in range(CONV_WIDTH - 1):
        y = y + conv_w[j] * hp[:, j:j + seq]
    a, b = jnp.split(y, 2, axis=-1)
    return (jax.nn.silu(a) * b) @ w_down


def setup_inputs(seed: int = 0) -> dict:
    key = jax.random.key(seed)
    ks = jax.random.split(key, 19)

    def nrm(k, shape, scale):
        return jax.random.normal(k, shape, F32) * scale

    def gain(k, shape):
        return 1.0 + 0.05 * jax.random.normal(k, shape, F32)

    return {
        'x': nrm(ks[0], (BATCH, SEQ, D_MODEL), 1.0),
        'rel_bias': nrm(ks[1], (REL_BUCKETS, A_HEADS), 0.5),
        'hgrn_lb': nrm(ks[2], (DEPTH, C_HEADS * C_KEY_DIM), 0.5),
        'mix_norm': gain(ks[3], (DEPTH, D_MODEL)),
        'ffn_norm': gain(ks[4], (DEPTH, D_MODEL)),
        'final_norm': gain(ks[5], (D_MODEL,)),
        'ab_w_in': nrm(ks[6], (N_EVEN, D_MODEL, AB_IN), D_MODEL ** -0.5),
        'ab_idx_k_norm': gain(ks[7], (N_EVEN, IDX_DIM)),
        'ab_gmlp_norm': gain(ks[8], (N_EVEN, B_WIDTH)),
        'ab_w_s': nrm(ks[9], (N_EVEN, B_GROUPS, B_CHUNK, B_CHUNK), B_CHUNK ** -0.5),
        'ab_b_s': 1.0 + nrm(ks[10], (N_EVEN, B_GROUPS, B_CHUNK), 0.02),
        'ab_w_out': nrm(ks[11], (N_EVEN, A_WIDTH + B_WIDTH, D_MODEL), (A_WIDTH + B_WIDTH) ** -0.5),
        'c_w_in': nrm(ks[12], (N_ODD, D_MODEL, C_IN), D_MODEL ** -0.5),
        'c_out_norm': gain(ks[13], (N_ODD, C_VAL_DIM)),
        'c_w_out': nrm(ks[14], (N_ODD, C_WIDTH, D_MODEL), C_WIDTH ** -0.5),
        'ffn_w_up': nrm(ks[15], (DEPTH, D_MODEL, 2 * D_FF), D_MODEL ** -0.5),
        'ffn_conv_w': nrm(ks[16], (DEPTH, CONV_WIDTH, 2 * D_FF), CONV_WIDTH ** -0.5),
        'ffn_conv_b': nrm(ks[17], (DEPTH, 2 * D_FF), 0.02),
        'ffn_w_down': nrm(ks[18], (DEPTH, D_FF, D_MODEL), D_FF ** -0.5),
    }


def reference(x, rel_bias, hgrn_lb, mix_norm, ffn_norm, final_norm,
              ab_w_in, ab_idx_k_norm, ab_gmlp_norm, ab_w_s, ab_b_s, ab_w_out,
              c_w_in, c_out_norm, c_w_out,
              ffn_w_up, ffn_conv_w, ffn_conv_b, ffn_w_down):
    bsz, seq, _ = x.shape
    lb_all = jnp.cumsum(jax.nn.softmax(hgrn_lb.astype(F32), axis=0), axis=0)
    lb_all = lb_all - lb_all[0:1]
    for l in range(DEPTH):
        h = rms_norm(x, mix_norm[l])
        if l % 2 == 0:
            e = l // 2
            q, k, v, iq, ik, iw, u, vg = split_cols(h @ ab_w_in[e], AB_SPLITS)
            ik = rms_norm(ik, ab_idx_k_norm[e])
            y_a = dsa_attention(q.reshape(bsz, seq, A_HEADS, A_HEAD_DIM),
                                k.reshape(bsz, seq, A_HEADS, A_HEAD_DIM),
                                v.reshape(bsz, seq, A_HEADS, A_HEAD_DIM),
                                iq.reshape(bsz, seq, IDX_HEADS, IDX_DIM),
                                ik, iw, rel_bias)
            y_b = gmlp_spatial_gate(jax.nn.gelu(u), jax.nn.gelu(vg),
                                    ab_gmlp_norm[e], ab_w_s[e], ab_b_s[e])
            y = jnp.concatenate([y_a, y_b], axis=-1) @ ab_w_out[e]
        else:
            o_i = l // 2
            q, f, i, gate = split_cols(h @ c_w_in[o_i], C_SPLITS)
            o = hgrn2(q, f, i, lb_all[l])
            o = rms_norm(o, c_out_norm[o_i]).reshape(bsz, seq, C_WIDTH).astype(x.dtype)
            y = (o * jax.nn.silu(gate)) @ c_w_out[o_i]
        x = x + y
        x = x + conv_ffn(rms_norm(x, ffn_norm[l]), ffn_w_up[l], ffn_conv_w[l],
                         ffn_conv_b[l], ffn_w_down[l])
    return rms_norm(x, final_norm)
```

```python
import functools
import math

import jax
import jax.numpy as jnp
from jax import lax
from jax.experimental import pallas as pl
from jax.experimental.pallas import tpu as pltpu

F32 = jnp.float32
BF16 = jnp.bfloat16
I32 = jnp.int32

D_MODEL = 1024
CHUNK = 64
A_HEADS = 8
A_HEAD_DIM = 64
A_WIDTH = A_HEADS * A_HEAD_DIM
IDX_HEADS = 16
IDX_DIM = 64
TOPK_MAX = 256
REL_BUCKETS = 32
REL_MAX_DIST = 128
B_GROUPS = 8
B_GROUP_DIM = 64
B_WIDTH = B_GROUPS * B_GROUP_DIM
B_CHUNK = 128
C_HEADS = 8
C_KEY_DIM = 128
C_VAL_DIM = 128
C_WIDTH = C_HEADS * C_VAL_DIM
D_FF = 2816
EPS = 1e-6

LANES = 128
VMEM_LIMIT = 56 * 1024 * 1024

ROW_TILE = 512
DSA_TILE = 256
HG_TILE = 256
HG_SUB = 16
FF_COLS = 256
NEG = -1e30
M_INIT = -1e29
INT_MIN = -2 ** 31

_C_Q, _C_K, _C_V, _C_IQ, _C_U, _C_VG, _C_IK, _C_IW, _C_END = (
    0, 512, 1024, 1536, 2560, 3072, 3584, 3712, 3840)


def _params(sem):
    return pltpu.CompilerParams(dimension_semantics=sem, vmem_limit_bytes=VMEM_LIMIT)


def _resident(shape):
    nd = len(shape)
    return pl.BlockSpec(shape, lambda *_: (0,) * nd, pipeline_mode=pl.Buffered(1))


def _rms(x, g):
    ms = jnp.mean(x * x, axis=-1, keepdims=True)
    return x * lax.rsqrt(ms + EPS) * g


def _dot(a, b):
    return jnp.dot(a, b, preferred_element_type=F32)


def _dot_nt(a, b):
    return lax.dot_general(a, b, (((1,), (1,)), ((), ())), preferred_element_type=F32)


def _dot_tn(a, b):
    return lax.dot_general(a, b, (((0,), (0,)), ((), ())), preferred_element_type=F32)


def _t5_bucket(rel):
    half = REL_BUCKETS // 2
    max_exact = half // 2
    ret = jnp.where(rel > 0, half, 0)
    n = jnp.abs(rel)
    nf = jnp.maximum(n, max_exact).astype(F32)
    large = max_exact + (jnp.log(nf / max_exact) / math.log(REL_MAX_DIST / max_exact)
                         * (half - max_exact)).astype(I32)
    large = jnp.minimum(large, half - 1)
    return ret + jnp.where(n < max_exact, n, large)


def _bias_kernel(rb_ref, bucket_ref, o_ref):
    h = pl.program_id(1)
    bucket = bucket_ref[0]
    tile = jnp.zeros(bucket.shape, F32)
    for b in range(REL_BUCKETS):
        tile = jnp.where(bucket == b, rb_ref[b, h], tile)
    o_ref[0, 0] = tile


def _bias_tiles(rel_bias):
    t = DSA_TILE
    assert t >= REL_MAX_DIST
    qq = jnp.arange(t, dtype=I32)[:, None]
    kk = jnp.arange(t, dtype=I32)[None, :]
    rel = jnp.stack([kk - qq, kk - qq - t, jnp.full((t, t), -2 * t, I32)])
    bucket = _t5_bucket(rel)
    return pl.pallas_call(
        _bias_kernel,
        grid=(3, A_HEADS),
        in_specs=[pl.BlockSpec(memory_space=pltpu.SMEM),
                  pl.BlockSpec((1, t, t), lambda d, h: (d, 0, 0))],
        out_specs=pl.BlockSpec((1, 1, t, t), lambda d, h: (d, h, 0, 0)),
        out_shape=jax.ShapeDtypeStruct((3, A_HEADS, t, t), F32),
        compiler_params=_params(("arbitrary", "arbitrary")),
        name="bias_tiles",
    )(rel_bias, bucket)


def _proj_ab_kernel(x_ref, g_ref, w_ref, gik_ref, ggm_ref, ws_ref, bs_ref,
                    q_ref, k_ref, v_ref, iq_ref, ik_ref, iw_ref, yb_ref):
    tm = x_ref.shape[0]
    h = _rms(x_ref[...], g_ref[...]).astype(BF16)

    def proj(c0, c1):
        return _dot(h, w_ref[:, c0:c1])

    q_ref[...] = (proj(_C_Q, _C_K) * (A_HEAD_DIM ** -0.5)).astype(BF16)
    k_ref[...] = proj(_C_K, _C_V).astype(BF16)
    v_ref[...] = proj(_C_V, _C_IQ).astype(BF16)
    half = (_C_U - _C_IQ) // 2
    for j in range(2):
        iq_ref[:, j * half:(j + 1) * half] = (
            proj(_C_IQ + j * half, _C_IQ + (j + 1) * half) * (IDX_DIM ** -0.5)).astype(BF16)
    ik_ref[...] = _rms(proj(_C_IK, _C_IW), gik_ref[...]).astype(BF16)
    iw_ref[...] = proj(_C_IW, _C_END)[:, :IDX_HEADS] * (IDX_HEADS ** -0.5)

    u = jax.nn.gelu(proj(_C_U, _C_VG))
    vn = _rms(jax.nn.gelu(proj(_C_VG, _C_IK)), ggm_ref[...]).astype(BF16)
    ri = lax.broadcasted_iota(I32, (B_CHUNK, B_CHUNK), 0) // CHUNK
    ci = lax.broadcasted_iota(I32, (B_CHUNK, B_CHUNK), 1) // CHUNK
    low = ri >= ci
    first = lax.broadcasted_iota(I32, (B_CHUNK, LANES), 1) < B_GROUP_DIM
    for p in range(B_GROUPS // 2):
        w_e = jnp.where(low, ws_ref[2 * p], 0.0).astype(BF16)
        w_o = jnp.where(low, ws_ref[2 * p + 1], 0.0).astype(BF16)
        b_e = bs_ref[:, 2 * p:2 * p + 1]
        b_o = bs_ref[:, 2 * p + 1:2 * p + 2]
        cols = slice(p * LANES, (p + 1) * LANES)
        for r in range(tm // B_CHUNK):
            rows = slice(r * B_CHUNK, (r + 1) * B_CHUNK)
            vp = vn[rows, cols]
            s = jnp.where(first, _dot(w_e, vp) + b_e, _dot(w_o, vp) + b_o)
            yb_ref[rows, cols] = (u[rows, cols] * s).astype(BF16)


def _proj_ab(x2, g, w, gik, ggm, ws, bs_t):
    m = x2.shape[0]
    tm = ROW_TILE
    row = lambda n: pl.BlockSpec((tm, n), lambda i: (i, 0))
    outs = [(A_WIDTH, BF16), (A_WIDTH, BF16), (A_WIDTH, BF16), (IDX_HEADS * IDX_DIM, BF16),
            (2 * IDX_DIM, BF16), (IDX_HEADS, F32), (B_WIDTH, BF16)]
    return pl.pallas_call(
        _proj_ab_kernel,
        grid=(m // tm,),
        in_specs=[row(D_MODEL), _resident(g.shape), _resident(w.shape), _resident(gik.shape),
                  _resident(ggm.shape), _resident(ws.shape), _resident(bs_t.shape)],
        out_specs=[row(n) for n, _ in outs],
        out_shape=[jax.ShapeDtypeStruct((m, n), dt) for n, dt in outs],
        compiler_params=_params(("parallel",)),
        name="proj_ab",
    )(x2, g, w, gik, ggm, ws, bs_t)


def _dsa_kernel(q_ref, k_ref, v_ref, iq_ref, ik_ref, iw_ref, bias_ref, o_ref,
                keys_scr, madd_scr, iqm_scr, qm_scr, *, top_k):
    t = DSA_TILE
    i = pl.program_id(1)
    nlg = t // LANES
    first = lax.broadcasted_iota(I32, (t, LANES), 1) < A_HEAD_DIM
    allowed = (lax.broadcasted_iota(I32, (t, t), 1) // CHUNK
               <= lax.broadcasted_iota(I32, (t, t), 0) // CHUNK)

    def blk(kb):
        return pl.ds(pl.multiple_of(kb * t, t), t)

    for h in range(IDX_HEADS):
        pair = iq_ref[:, (h // 2) * LANES:(h // 2 + 1) * LANES]
        iqm_scr[h] = jnp.where(first if h % 2 == 0 else ~first, pair, jnp.zeros_like(pair))
    for h in range(A_HEADS):
        pair = q_ref[:, (h // 2) * LANES:(h // 2 + 1) * LANES]
        qm_scr[h] = jnp.where(first if h % 2 == 0 else ~first, pair, jnp.zeros_like(pair))

    def score_keys(kb, diag):
        ikb = ik_ref[blk(kb), :]
        acc = jnp.zeros((t, t), F32)
        for h in range(IDX_HEADS):
            s = _dot_nt(iqm_scr[h], ikb)
            acc = acc + jnp.maximum(s, 0.0) * iw_ref[:, h:h + 1]
        bits = lax.bitcast_convert_type(acc, I32)
        key = bits ^ ((bits >> 31) & jnp.int32(0x7FFFFFFF))
        if diag:
            key = jnp.where(allowed, key, jnp.int32(INT_MIN))
        keys_scr[:, blk(kb)] = key

    def score_body(kb, c):
        score_keys(kb, False)
        return c

    lax.fori_loop(0, i, score_body, 0)
    score_keys(i, True)

    @pl.when(i == 0)
    def _():
        madd_scr[:, 0:t] = jnp.where(allowed, 0.0, NEG)

    @pl.when(i > 0)
    def _():
        def count_ge(thr):
            def body(kb, cnt):
                kk = keys_scr[:, blk(kb)]
                for g in range(nlg):
                    cnt = cnt + jnp.where(kk[:, g * LANES:(g + 1) * LANES] >= thr, 1.0, 0.0)
                return cnt
            cnt = lax.fori_loop(0, i + 1, body, jnp.zeros((t, LANES), F32))
            return jnp.sum(cnt, axis=1, keepdims=True)

        def bit_body(it, thr):
            cand = thr + (jnp.int32(1) << (31 - it))
            return jnp.where(count_ge(cand) >= top_k, cand, thr)

        thr = lax.fori_loop(0, 32, bit_body, jnp.full((t, LANES), INT_MIN, I32))
        n_ge = count_ge(thr)

        def plain_body(kb, c):
            madd_scr[:, blk(kb)] = jnp.where(
                keys_scr[:, blk(kb)] >= jnp.concatenate([thr] * nlg, axis=1), 0.0, NEG)
            return c

        lax.fori_loop(0, i + 1, plain_body, 0)

        @pl.when(jnp.max(n_ge) > top_k)
        def _():
            def gt_body(kb, cnt):
                kk = keys_scr[:, blk(kb)]
                for g in range(nlg):
                    cnt = cnt + jnp.where(kk[:, g * LANES:(g + 1) * LANES] > thr, 1.0, 0.0)
                return cnt
            n_gt = jnp.sum(lax.fori_loop(0, i + 1, gt_body, jnp.zeros((t, LANES), F32)),
                           axis=1, keepdims=True)
            need = top_k - n_gt
            upper = (lax.broadcasted_iota(I32, (LANES, LANES), 0)
                     < lax.broadcasted_iota(I32, (LANES, LANES), 1))
            upper = jnp.where(upper, 1.0, 0.0).astype(BF16)

            def tie_body(kb, run):
                kk = keys_scr[:, blk(kb)]
                cols = []
                for g in range(nlg):
                    kg = kk[:, g * LANES:(g + 1) * LANES]
                    eq = jnp.where(kg == thr, 1.0, 0.0)
                    rank = run + _dot(eq.astype(BF16), upper)
                    sel = (kg > thr) | ((kg == thr) & (rank < need))
                    cols.append(jnp.where(sel, 0.0, NEG))
                    run = run + jnp.sum(eq, axis=1, keepdims=True)
                madd_scr[:, blk(kb)] = jnp.concatenate(cols, axis=1)
                return run

            lax.fori_loop(0, i + 1, tie_body, jnp.zeros((t, 1), F32))

    for p in range(A_HEADS // 2):
        cols = slice(p * LANES, (p + 1) * LANES)

        def att_block(kb, carry, didx, p=p, cols=cols):
            madd = madd_scr[:, blk(kb)]
            kp = k_ref[blk(kb), cols]
            vp = v_ref[blk(kb), cols]
            new = []
            pv = []
            alphas = []
            for e in range(2):
                m, l = carry[2 * e], carry[2 * e + 1]
                h = 2 * p + e
                s = _dot_nt(qm_scr[h], kp) + bias_ref[didx, h] + madd
                m_new = jnp.maximum(m, jnp.max(s, axis=1, keepdims=True))
                alpha = jnp.exp(m - m_new)
                pr = jnp.exp(s - m_new)
                l_new = alpha * l + jnp.sum(pr, axis=1, keepdims=True)
                pv.append(_dot(pr.astype(BF16), vp))
                alphas.append(alpha)
                new += [m_new, l_new]
            acc = carry[4]
            acc = (jnp.where(first, alphas[0], alphas[1]) * acc
                   + jnp.where(first, pv[0], pv[1]))
            return tuple(new) + (acc,)

        init = (jnp.full((t, 1), M_INIT, F32), jnp.zeros((t, 1), F32),
                jnp.full((t, 1), M_INIT, F32), jnp.zeros((t, 1), F32),
                jnp.zeros((t, LANES), F32))
        carry = lax.fori_loop(
            0, i, lambda kb, c: att_block(kb, c, jnp.minimum(i - kb, 2)), init)
        carry = att_block(i, carry, 0)
        inv = jnp.where(first, 1.0 / carry[1], 1.0 / carry[3])
        o_ref[:, cols] = (carry[4] * inv).astype(BF16)


def _dsa(q, k, v, iq, ik2, iw, bias, bsz, seq):
    t = DSA_TILE
    nq = seq // t
    top_k = min(TOPK_MAX, seq // 4)
    assert t <= top_k and t % CHUNK == 0 and seq % t == 0
    qrow = lambda n: pl.BlockSpec((t, n), lambda b, i: (b * nq + i, 0))
    full = lambda n: pl.BlockSpec((seq, n), lambda b, i: (b, 0))
    return pl.pallas_call(
        functools.partial(_dsa_kernel, top_k=top_k),
        grid=(bsz, nq),
        in_specs=[qrow(A_WIDTH), full(A_WIDTH), full(A_WIDTH), qrow(IDX_HEADS * IDX_DIM),
                  full(2 * IDX_DIM), qrow(IDX_HEADS), _resident(bias.shape)],
        out_specs=qrow(A_WIDTH),
        out_shape=jax.ShapeDtypeStruct((bsz * seq, A_WIDTH), BF16),
        scratch_shapes=[pltpu.VMEM((t, seq), I32), pltpu.VMEM((t, seq), F32),
                        pltpu.VMEM((IDX_HEADS, t, LANES), BF16),
                        pltpu.VMEM((A_HEADS, t, LANES), BF16)],
        compiler_params=_params(("parallel", "arbitrary")),
        name="dsa",
    )(q, k, v, iq, ik2, iw, bias)


def _outproj_kernel(*refs):
    x_ref, w_ref, o_ref = refs[0], refs[-2], refs[-1]
    acc = x_ref[...]
    r0 = 0
    for y_ref in refs[1:-2]:
        n = y_ref.shape[1]
        acc = acc + _dot(y_ref[...], w_ref[r0:r0 + n, :])
        r0 += n
    o_ref[...] = acc


def _outproj(x2, ys, w):
    m = x2.shape[0]
    tm = ROW_TILE
    row = lambda n: pl.BlockSpec((tm, n), lambda i: (i, 0))
    return pl.pallas_call(
        _outproj_kernel,
        grid=(m // tm,),
        in_specs=[row(D_MODEL)] + [row(y.shape[1]) for y in ys] + [_resident(w.shape)],
        out_specs=row(D_MODEL),
        out_shape=jax.ShapeDtypeStruct((m, D_MODEL), F32),
        compiler_params=_params(("parallel",)),
        name="outproj",
    )(x2, *ys, w)


HALO = 8


def _ffn_kernel(x_ref, halo_ref, g_ref, wup_ref, cw_ref, cb_ref, wdn_ref, gfin_ref, o_ref,
                ua_scr, ub_scr, acc_scr, *, tiles_per_seq, final_norm):
    tm = x_ref.shape[0]
    x = x_ref[...]
    hn = _rms(x, g_ref[...])
    keep = jnp.where(pl.program_id(0) % tiles_per_seq == 0, 0.0, 1.0)
    hh = _rms(halo_ref[...], g_ref[...]) * keep
    he = jnp.concatenate([hh, hn], axis=0).astype(BF16)

    def conv(scr, c0):
        w = cw_ref[:, c0:c0 + FF_COLS]
        return (cb_ref[:, c0:c0 + FF_COLS]
                + w[2:3] * scr[HALO:HALO + tm]
                + w[1:2] * scr[HALO - 1:HALO - 1 + tm]
                + w[0:1] * scr[HALO - 2:HALO - 2 + tm])

    for j in range(D_FF // FF_COLS):
        ca, cb = j * FF_COLS, D_FF + j * FF_COLS
        ua_scr[...] = _dot(he, wup_ref[:, ca:ca + FF_COLS])
        ub_scr[...] = _dot(he, wup_ref[:, cb:cb + FF_COLS])
        act = (jax.nn.silu(conv(ua_scr, ca)) * conv(ub_scr, cb)).astype(BF16)
        part = _dot(act, wdn_ref[ca:ca + FF_COLS, :])
        if j == 0:
            acc_scr[...] = part
        else:
            acc_scr[...] += part
    y = x + acc_scr[...]
    if final_norm:
        y = _rms(y, gfin_ref[...])
    o_ref[...] = y


def _ffn(x2, g, wup, cw, cb, wdn, gfin, seq, final_norm):
    m = x2.shape[0]
    tm = ROW_TILE
    assert seq % tm == 0 and D_FF % FF_COLS == 0 and tm % HALO == 0
    row = pl.BlockSpec((tm, D_MODEL), lambda i: (i, 0))
    halo = pl.BlockSpec((HALO, D_MODEL), lambda i: (jnp.maximum(i * (tm // HALO) - 1, 0), 0))
    return pl.pallas_call(
        functools.partial(_ffn_kernel, tiles_per_seq=seq // tm, final_norm=final_norm),
        grid=(m // tm,),
        in_specs=[row, halo, _resident(g.shape), _resident(wup.shape), _resident(cw.shape),
                  _resident(cb.shape), _resident(wdn.shape), _resident(gfin.shape)],
        out_specs=row,
        out_shape=jax.ShapeDtypeStruct((m, D_MODEL), F32),
        scratch_shapes=[pltpu.VMEM((tm + HALO, FF_COLS), F32),
                        pltpu.VMEM((tm + HALO, FF_COLS), F32),
                        pltpu.VMEM((tm, D_MODEL), F32)],
        compiler_params=_params(("parallel",)),
        name="conv_ffn",
    )(x2, x2, g, wup, cw, cb, wdn, gfin)


def _hgrn_kernel(x_ref, g_ref, w_ref, lbp_ref, gout_ref, o_ref, proj_scr, st_scr, *, layer):
    ts = x_ref.shape[0]
    kw = C_HEADS * C_KEY_DIM
    c = CHUNK
    nsub = c // HG_SUB

    @pl.when(pl.program_id(1) == 0)
    def _():
        st_scr[...] = jnp.zeros(st_scr.shape, F32)

    h = _rms(x_ref[...], g_ref[...]).astype(BF16)
    step = 512
    for n in range(proj_scr.shape[1] // step):
        proj_scr[:, n * step:(n + 1) * step] = _dot(h, w_ref[:, n * step:(n + 1) * step])

    lbp = lbp_ref[...]
    e = jnp.exp(lbp - jnp.max(lbp, axis=0, keepdims=True))
    sm = e / jnp.sum(e, axis=0, keepdims=True)
    lb = jnp.sum(sm[1:layer + 1], axis=0, keepdims=True) if layer > 0 else jnp.zeros((1, kw), F32)

    ri = lax.broadcasted_iota(I32, (c, c), 0)
    ci = lax.broadcasted_iota(I32, (c, c), 1)
    r3 = lax.broadcasted_iota(I32, (c, 3 * c), 0)
    c3 = lax.broadcasted_iota(I32, (c, 3 * c), 1) % c
    tri3 = jnp.where(r3 >= c3, 1.0, 0.0).astype(BF16)
    dmask = (ri >= ci) & (ri // HG_SUB == ci // HG_SUB)

    def chunk_body(ic, carry):
        rows = pl.ds(pl.multiple_of(ic * c, c), c)
        gg = lb + (1.0 - lb) * jax.nn.sigmoid(proj_scr[rows, kw:2 * kw])
        lg = jnp.log(gg)
        hi = lg.astype(BF16)
        r1 = lg - hi.astype(F32)
        mid = r1.astype(BF16)
        lo = (r1 - mid.astype(F32)).astype(BF16)
        b_all = _dot(tri3, jnp.concatenate([hi, mid, lo], axis=0))
        for hd in range(C_HEADS):
            kc = slice(hd * C_KEY_DIM, (hd + 1) * C_KEY_DIM)
            vc = slice(hd * C_VAL_DIM, (hd + 1) * C_VAL_DIM)
            q = jax.nn.silu(proj_scr[rows, hd * C_KEY_DIM:(hd + 1) * C_KEY_DIM])
            kk = 1.0 - gg[:, kc]
            b = b_all[:, kc]
            iv = proj_scr[rows, 2 * kw + hd * C_VAL_DIM:2 * kw + (hd + 1) * C_VAL_DIM].astype(BF16)
            gate = proj_scr[rows, 2 * kw + C_WIDTH + hd * C_VAL_DIM:
                            2 * kw + C_WIDTH + (hd + 1) * C_VAL_DIM]
            st = st_scr[hd]
            inter = _dot_nt((q * jnp.exp(b)).astype(BF16), st.astype(BF16))
            qs, ks = [], []
            for j in range(nsub - 1):
                e0, e1 = j * HG_SUB, (j + 1) * HG_SUB
                rj = b[e1 - 1:e1]
                qs.append(jnp.concatenate(
                    [jnp.zeros((e1, C_KEY_DIM), F32), q[e1:] * jnp.exp(b[e1:] - rj)], axis=0))
                kj = kk[e0:e1] * jnp.exp(rj - b[e0:e1])
                parts = [kj, jnp.zeros((c - e1, C_KEY_DIM), F32)]
                if e0:
                    parts = [jnp.zeros((e0, C_KEY_DIM), F32)] + parts
                ks.append(jnp.concatenate(parts, axis=0))
            a_off = _dot_nt(jnp.concatenate(qs, axis=1).astype(BF16),
                            jnp.concatenate(ks, axis=1).astype(BF16))
            bm = jnp.concatenate(
                [jnp.broadcast_to(b[j * HG_SUB + HG_SUB // 2:j * HG_SUB + HG_SUB // 2 + 1],
                                  (HG_SUB, C_KEY_DIM)) for j in range(nsub)], axis=0)
            a_dg = _dot_nt((q * jnp.exp(b - bm)).astype(BF16), (kk * jnp.exp(bm - b)).astype(BF16))
            a = a_off + jnp.where(dmask, a_dg, 0.0)
            o = inter + _dot(a.astype(BF16), iv)
            bl = b[c - 1:c]
            upd = _dot_tn(iv, (kk * jnp.exp(bl - b)).astype(BF16))
            st_scr[hd] = st * jnp.exp(bl) + upd
            on = _rms(o, gout_ref[...])
            o_ref[rows, vc] = (on * jax.nn.silu(gate)).astype(BF16)
        return carry

    lax.fori_loop(0, ts // c, chunk_body, 0)


def _hgrn(x2, g, w, lbp, gout, bsz, seq, layer):
    ts = HG_TILE
    nt = seq // ts
    assert seq % ts == 0 and ts % CHUNK == 0 and CHUNK % HG_SUB == 0
    row = lambda n: pl.BlockSpec((ts, n), lambda b, i: (b * nt + i, 0))
    return pl.pallas_call(
        functools.partial(_hgrn_kernel, layer=layer),
        grid=(bsz, nt),
        in_specs=[row(D_MODEL), _resident(g.shape), _resident(w.shape), _resident(lbp.shape),
                  _resident(gout.shape)],
        out_specs=row(C_WIDTH),
        out_shape=jax.ShapeDtypeStruct((bsz * seq, C_WIDTH), BF16),
        scratch_shapes=[pltpu.VMEM((ts, w.shape[1]), F32),
                        pltpu.VMEM((C_HEADS, C_VAL_DIM, C_KEY_DIM), F32)],
        compiler_params=_params(("parallel", "arbitrary")),
        name="hgrn2",
    )(x2, g, w, lbp, gout)


def kernel(x, rel_bias, hgrn_lb, mix_norm, ffn_norm, final_norm, ab_w_in, ab_idx_k_norm, ab_gmlp_norm, ab_w_s, ab_b_s, ab_w_out, c_w_in, c_out_norm, c_w_out, ffn_w_up, ffn_conv_w, ffn_conv_b, ffn_w_down):
    bsz, seq, d = x.shape
    depth = mix_norm.shape[0]
    x2 = x.reshape(bsz * seq, d)
    row = lambda a: a.reshape(1, -1)
    bias = _bias_tiles(rel_bias)
    for l in range(depth):
        if l % 2 == 0:
            e = l // 2
            w = ab_w_in[e]
            offs = [0]
            for s in (A_WIDTH, A_WIDTH, A_WIDTH, IDX_HEADS * IDX_DIM, IDX_DIM, IDX_HEADS,
                      B_WIDTH, B_WIDTH):
                offs.append(offs[-1] + s)
            seg = [w[:, offs[n]:offs[n + 1]] for n in range(8)]
            w_p = jnp.concatenate(
                seg[0:4] + seg[6:8] + [seg[4], seg[4], seg[5],
                                       jnp.zeros((d, LANES - IDX_HEADS), w.dtype)],
                axis=1).astype(BF16)
            assert w_p.shape[1] == _C_END
            gik = jnp.concatenate([ab_idx_k_norm[e], ab_idx_k_norm[e]])
            q, k, v, iq, ik2, iw, y_b = _proj_ab(
                x2, row(mix_norm[l]), w_p, row(gik), row(ab_gmlp_norm[e]),
                ab_w_s[e], jnp.transpose(ab_b_s[e]))
            y_a = _dsa(q, k, v, iq, ik2, iw, bias, bsz, seq)
            x2 = _outproj(x2, [y_a, y_b], ab_w_out[e].astype(BF16))
        else:
            o_i = l // 2
            og = _hgrn(x2, row(mix_norm[l]), c_w_in[o_i].astype(BF16), hgrn_lb,
                       row(c_out_norm[o_i]), bsz, seq, l)
            x2 = _outproj(x2, [og], c_w_out[o_i].astype(BF16))
        x2 = _ffn(x2, row(ffn_norm[l]), ffn_w_up[l].astype(BF16), ffn_conv_w[l],
                  row(ffn_conv_b[l]), ffn_w_down[l].astype(BF16), row(final_norm),
                  seq, l == depth - 1)
    return x2.reshape(bsz, seq, d)
```

```python
import functools
import math

import jax
import jax.numpy as jnp
from jax import lax
from jax.experimental import pallas as pl
from jax.experimental.pallas import tpu as pltpu

F32 = jnp.float32
BF16 = jnp.bfloat16
I32 = jnp.int32

D_MODEL = 1024
CHUNK = 64
A_HEADS = 8
A_HEAD_DIM = 64
A_WIDTH = A_HEADS * A_HEAD_DIM
IDX_HEADS = 16
IDX_DIM = 64
TOPK_MAX = 256
REL_BUCKETS = 32
REL_MAX_DIST = 128
B_GROUPS = 8
B_GROUP_DIM = 64
B_WIDTH = B_GROUPS * B_GROUP_DIM
B_CHUNK = 128
C_HEADS = 8
C_KEY_DIM = 128
C_VAL_DIM = 128
C_WIDTH = C_HEADS * C_VAL_DIM
D_FF = 2816
EPS = 1e-6

LANES = 128
VMEM_LIMIT = 56 * 1024 * 1024

ROW_TILE = 512
DSA_TILE = 256
HG_TILE = 256
HG_SUB = 16
FF_COLS = 256
NEG = -1e30
M_INIT = -1e29
INT_MIN = -2 ** 31

_C_Q, _C_K, _C_V, _C_IQ, _C_U, _C_VG, _C_IK, _C_IW, _C_END = (
    0, 512, 1024, 1536, 2560, 3072, 3584, 3712, 3840)


def _params(sem):
    return pltpu.CompilerParams(dimension_semantics=sem, vmem_limit_bytes=VMEM_LIMIT)


def _resident(shape):
    nd = len(shape)
    return pl.BlockSpec(shape, lambda *_: (0,) * nd, pipeline_mode=pl.Buffered(1))


def _rms(x, g):
    ms = jnp.mean(x * x, axis=-1, keepdims=True)
    return x * lax.rsqrt(ms + EPS) * g


def _dot(a, b):
    return jnp.dot(a, b, preferred_element_type=F32)


def _dot_nt(a, b):
    return lax.dot_general(a, b, (((1,), (1,)), ((), ())), preferred_element_type=F32)


def _dot_tn(a, b):
    return lax.dot_general(a, b, (((0,), (0,)), ((), ())), preferred_element_type=F32)


def _t5_bucket(rel):
    half = REL_BUCKETS // 2
    max_exact = half // 2
    ret = jnp.where(rel > 0, half, 0)
    n = jnp.abs(rel)
    nf = jnp.maximum(n, max_exact).astype(F32)
    large = max_exact + (jnp.log(nf / max_exact) / math.log(REL_MAX_DIST / max_exact)
                         * (half - max_exact)).astype(I32)
    large = jnp.minimum(large, half - 1)
    return ret + jnp.where(n < max_exact, n, large)


def _bias_kernel(rb_ref, bucket_ref, far_ref, o_ref):
    h = pl.program_id(1)

    def lookup(bucket):
        tile = jnp.zeros(bucket.shape, F32)
        for b in range(REL_BUCKETS):
            tile = jnp.where(bucket == b, rb_ref[b, h], tile)
        return tile

    o_ref[0, 0] = lookup(bucket_ref[0]) - lookup(far_ref[0])


def _bias_tiles(rel_bias):
    t = DSA_TILE
    assert t >= REL_MAX_DIST
    kk = jnp.arange(t, dtype=I32)[:, None]
    qq = jnp.arange(t, dtype=I32)[None, :]
    rel = jnp.stack([kk - qq, kk - qq - t, jnp.full((t, t), -2 * t, I32)])
    bucket = _t5_bucket(rel)
    return pl.pallas_call(
        _bias_kernel,
        grid=(2, A_HEADS),
        in_specs=[pl.BlockSpec(memory_space=pltpu.SMEM),
                  pl.BlockSpec((1, t, t), lambda d, h: (d, 0, 0)),
                  pl.BlockSpec((1, t, t), lambda d, h: (2, 0, 0))],
        out_specs=pl.BlockSpec((1, 1, t, t), lambda d, h: (d, h, 0, 0)),
        out_shape=jax.ShapeDtypeStruct((2, A_HEADS, t, t), F32),
        compiler_params=_params(("arbitrary", "arbitrary")),
        name="bias_tiles",
    )(rel_bias, bucket, bucket)


def _proj_ab_kernel(x_ref, g_ref, w_ref, gik_ref, ggm_ref, ws_ref, bs_ref,
                    q_ref, k_ref, v_ref, iq_ref, ik_ref, iw_ref, yb_ref):
    tm = x_ref.shape[0]
    h = _rms(x_ref[...], g_ref[...]).astype(BF16)

    def proj(c0, c1):
        return _dot(h, w_ref[:, c0:c1])

    q_ref[...] = (proj(_C_Q, _C_K) * (A_HEAD_DIM ** -0.5)).astype(BF16)
    k_ref[...] = proj(_C_K, _C_V).astype(BF16)
    v_ref[...] = proj(_C_V, _C_IQ).astype(BF16)
    half = (_C_U - _C_IQ) // 2
    for j in range(2):
        iq_ref[:, j * half:(j + 1) * half] = (
            proj(_C_IQ + j * half, _C_IQ + (j + 1) * half) * (IDX_DIM ** -0.5)).astype(BF16)
    ik_ref[...] = _rms(proj(_C_IK, _C_IW), gik_ref[...]).astype(BF16)
    iw_ref[...] = proj(_C_IW, _C_END)[:, :IDX_HEADS] * (IDX_HEADS ** -0.5)

    u = jax.nn.gelu(proj(_C_U, _C_VG))
    vn = _rms(jax.nn.gelu(proj(_C_VG, _C_IK)), ggm_ref[...]).astype(BF16)
    ri = lax.broadcasted_iota(I32, (B_CHUNK, B_CHUNK), 0) // CHUNK
    ci = lax.broadcasted_iota(I32, (B_CHUNK, B_CHUNK), 1) // CHUNK
    low = ri >= ci
    first = lax.broadcasted_iota(I32, (B_CHUNK, LANES), 1) < B_GROUP_DIM
    for p in range(B_GROUPS // 2):
        w_e = jnp.where(low, ws_ref[2 * p], 0.0).astype(BF16)
        w_o = jnp.where(low, ws_ref[2 * p + 1], 0.0).astype(BF16)
        b_e = bs_ref[:, 2 * p:2 * p + 1]
        b_o = bs_ref[:, 2 * p + 1:2 * p + 2]
        cols = slice(p * LANES, (p + 1) * LANES)
        for r in range(tm // B_CHUNK):
            rows = slice(r * B_CHUNK, (r + 1) * B_CHUNK)
            vp = vn[rows, cols]
            s = jnp.where(first, _dot(w_e, vp) + b_e, _dot(w_o, vp) + b_o)
            yb_ref[rows, cols] = (u[rows, cols] * s).astype(BF16)


def _proj_ab(x2, g, w, gik, ggm, ws, bs_t):
    m = x2.shape[0]
    tm = ROW_TILE
    row = lambda n: pl.BlockSpec((tm, n), lambda i: (i, 0))
    outs = [(A_WIDTH, BF16), (A_WIDTH, BF16), (A_WIDTH, BF16), (IDX_HEADS * IDX_DIM, BF16),
            (2 * IDX_DIM, BF16), (IDX_HEADS, F32), (B_WIDTH, BF16)]
    return pl.pallas_call(
        _proj_ab_kernel,
        grid=(m // tm,),
        in_specs=[row(D_MODEL), _resident(g.shape), _resident(w.shape), _resident(gik.shape),
                  _resident(ggm.shape), _resident(ws.shape), _resident(bs_t.shape)],
        out_specs=[row(n) for n, _ in outs],
        out_shape=[jax.ShapeDtypeStruct((m, n), dt) for n, dt in outs],
        compiler_params=_params(("parallel",)),
        name="proj_ab",
    )(x2, g, w, gik, ggm, ws, bs_t)


def _dsa_kernel(q_ref, k_ref, vt_ref, iq_ref, ik_ref, iwt_ref, bias_ref, o_ref,
                keys_scr, madd_scr, iqm_scr, qm_scr, ot_scr, m_scr, alpha_scr, acc_scr, s_scr,
                *, top_k):
    t = DSA_TILE
    sub = 8
    i = pl.program_id(1)
    first = lax.broadcasted_iota(I32, (t, LANES), 1) < A_HEAD_DIM
    allowed = (lax.broadcasted_iota(I32, (t, t), 0) // CHUNK
               <= lax.broadcasted_iota(I32, (t, t), 1) // CHUNK)

    def blk(kb):
        return pl.ds(pl.multiple_of(kb * t, t), t)

    for h in range(IDX_HEADS):
        pair = iq_ref[:, (h // 2) * LANES:(h // 2 + 1) * LANES]
        iqm_scr[h] = jnp.where(first if h % 2 == 0 else ~first, pair, jnp.zeros_like(pair))
    for h in range(A_HEADS):
        pair = q_ref[:, (h // 2) * LANES:(h // 2 + 1) * LANES]
        qm_scr[h] = jnp.where(first if h % 2 == 0 else ~first, pair, jnp.zeros_like(pair))

    def score_keys(kb, diag):
        ikb = ik_ref[blk(kb), :]
        acc = jnp.zeros((t, t), F32)
        for h in range(IDX_HEADS):
            s = _dot_nt(ikb, iqm_scr[h])
            acc = acc + jnp.maximum(s, 0.0) * iwt_ref[h:h + 1, :]
        bits = lax.bitcast_convert_type(acc, I32)
        key = bits ^ ((bits >> 31) & jnp.int32(0x7FFFFFFF))
        if diag:
            key = jnp.where(allowed, key, jnp.int32(INT_MIN))
        keys_scr[blk(kb), :] = key

    def score_body(kb, c):
        score_keys(kb, False)
        return c

    lax.fori_loop(0, i, score_body, 0)
    score_keys(i, True)

    @pl.when(i == 0)
    def _():
        madd_scr[0:t, :] = jnp.where(allowed, 0.0, NEG)

    @pl.when(i > 0)
    def _():
        def count(cmp, thr):
            def body(kb, cnt):
                kk = keys_scr[blk(kb), :]
                for r in range(t // sub):
                    cnt = cnt + jnp.where(cmp(kk[r * sub:(r + 1) * sub], thr), 1.0, 0.0)
                return cnt
            cnt = lax.fori_loop(0, i + 1, body, jnp.zeros((sub, t), F32))
            return jnp.sum(cnt, axis=0, keepdims=True)

        ge = lambda a, b: a >= b
        gt = lambda a, b: a > b

        def bit_body(it, thr):
            cand = thr + (jnp.int32(1) << (31 - it))
            return jnp.where(count(ge, cand) >= top_k, cand, thr)

        thr = lax.fori_loop(0, 32, bit_body, jnp.full((sub, t), INT_MIN, I32))
        n_ge = count(ge, thr)
        thr_row = thr[0:1]

        def plain_body(kb, c):
            madd_scr[blk(kb), :] = jnp.where(keys_scr[blk(kb), :] >= thr_row, 0.0, NEG)
            return c

        lax.fori_loop(0, i + 1, plain_body, 0)

        @pl.when(jnp.max(n_ge) > top_k)
        def _():
            need = top_k - count(gt, thr)
            lower = (lax.broadcasted_iota(I32, (t, t), 1)
                     < lax.broadcasted_iota(I32, (t, t), 0))
            lower = jnp.where(lower, 1.0, 0.0).astype(BF16)

            def tie_body(kb, run):
                kk = keys_scr[blk(kb), :]
                eq = jnp.where(kk == thr_row, 1.0, 0.0)
                rank = run + _dot(lower, eq.astype(BF16))
                sel = (kk > thr_row) | ((kk == thr_row) & (rank < need))
                madd_scr[blk(kb), :] = jnp.where(sel, 0.0, NEG)
                return run + jnp.sum(eq, axis=0, keepdims=True)

            lax.fori_loop(0, i + 1, tie_body, jnp.zeros((1, t), F32))

    ones = jnp.ones((SUM_ROWS, t), BF16)

    m_scr[...] = jnp.full(m_scr.shape, M_INIT, F32)
    acc_scr[...] = jnp.zeros(acc_scr.shape, F32)

    def att_block(kb, didx):
        @pl.when(kb <= i)
        def _():
            madd = madd_scr[blk(kb), :]
            for h in range(A_HEADS):
                cols = slice((h // 2) * LANES, (h // 2 + 1) * LANES)
                s = _dot_nt(k_ref[blk(kb), cols], qm_scr[h]) + madd
                if didx is not None:
                    s = s + bias_ref[didx, h]
                s_scr[h] = s
                m = m_scr[h]
                m_new = jnp.maximum(m, jnp.max(s, axis=0, keepdims=True))
                alpha_scr[h] = jnp.exp(m - m_new)
                m_scr[h] = m_new

        for h in range(A_HEADS):
            pr = jnp.exp(s_scr[h] - m_scr[h]).astype(BF16)
            v1 = jnp.concatenate(
                [vt_ref[h * A_HEAD_DIM:(h + 1) * A_HEAD_DIM, blk(kb)], ones], axis=0)
            acc_scr[h] = alpha_scr[h] * acc_scr[h] + _dot(v1, pr)

    def att_body(kb, c, didx):
        att_block(kb, didx)
        return c

    n_far = jnp.maximum(i - 1, 0)
    lax.fori_loop(0, n_far, functools.partial(att_body, didx=None), 0)
    lax.fori_loop(n_far, i, functools.partial(att_body, didx=1), 0)
    att_block(i, 0)
    for h in range(A_HEADS):
        acc = acc_scr[h]
        ot_scr[h * A_HEAD_DIM:(h + 1) * A_HEAD_DIM, :] = (
            acc[:A_HEAD_DIM] * (1.0 / acc[A_HEAD_DIM:A_HEAD_DIM + 1]))
    o_ref[...] = jnp.transpose(ot_scr[...]).astype(BF16)


SUM_ROWS = 16


def _dsa(q, k, vt, iq, ik2, iwt, bias, bsz, seq):
    t = DSA_TILE
    nq = seq // t
    top_k = min(TOPK_MAX, seq // 4)
    assert t <= top_k and t % CHUNK == 0 and seq % t == 0
    qrow = lambda n: pl.BlockSpec((t, n), lambda b, i: (b * nq + i, 0))
    full = lambda n: pl.BlockSpec((seq, n), lambda b, i: (b, 0))
    return pl.pallas_call(
        functools.partial(_dsa_kernel, top_k=top_k),
        grid=(bsz, nq),
        in_specs=[qrow(A_WIDTH), full(A_WIDTH),
                  pl.BlockSpec((A_WIDTH, seq), lambda b, i: (b, 0)),
                  qrow(IDX_HEADS * IDX_DIM), full(2 * IDX_DIM),
                  pl.BlockSpec((IDX_HEADS, t), lambda b, i: (0, b * nq + i)),
                  _resident(bias.shape)],
        out_specs=qrow(A_WIDTH),
        out_shape=jax.ShapeDtypeStruct((bsz * seq, A_WIDTH), BF16),
        scratch_shapes=[pltpu.VMEM((seq, t), I32), pltpu.VMEM((seq, t), F32),
                        pltpu.VMEM((IDX_HEADS, t, LANES), BF16),
                        pltpu.VMEM((A_HEADS, t, LANES), BF16),
                        pltpu.VMEM((A_WIDTH, t), F32),
                        pltpu.VMEM((A_HEADS, 1, t), F32),
                        pltpu.VMEM((A_HEADS, 1, t), F32),
                        pltpu.VMEM((A_HEADS, A_HEAD_DIM + SUM_ROWS, t), F32),
                        pltpu.VMEM((A_HEADS, t, t), F32)],
        compiler_params=_params(("parallel", "arbitrary")),
        name="dsa",
    )(q, k, vt, iq, ik2, iwt, bias)


def _outproj_kernel(*refs):
    x_ref, w_ref, o_ref = refs[0], refs[-2], refs[-1]
    acc = x_ref[...]
    r0 = 0
    for y_ref in refs[1:-2]:
        n = y_ref.shape[1]
        acc = acc + _dot(y_ref[...], w_ref[r0:r0 + n, :])
        r0 += n
    o_ref[...] = acc


def _outproj(x2, ys, w):
    m = x2.shape[0]
    tm = ROW_TILE
    row = lambda n: pl.BlockSpec((tm, n), lambda i: (i, 0))
    return pl.pallas_call(
        _outproj_kernel,
        grid=(m // tm,),
        in_specs=[row(D_MODEL)] + [row(y.shape[1]) for y in ys] + [_resident(w.shape)],
        out_specs=row(D_MODEL),
        out_shape=jax.ShapeDtypeStruct((m, D_MODEL), F32),
        compiler_params=_params(("parallel",)),
        name="outproj",
    )(x2, *ys, w)


HALO = 8


def _ffn_kernel(x_ref, halo_ref, g_ref, wup_ref, cw_ref, cb_ref, wdn_ref, gfin_ref, o_ref,
                ua_scr, ub_scr, acc_scr, *, tiles_per_seq, final_norm):
    tm = x_ref.shape[0]
    x = x_ref[...]
    hn = _rms(x, g_ref[...])
    keep = jnp.where(pl.program_id(0) % tiles_per_seq == 0, 0.0, 1.0)
    hh = _rms(halo_ref[...], g_ref[...]) * keep
    he = jnp.concatenate([hh, hn], axis=0).astype(BF16)

    def conv(scr, c0):
        w = cw_ref[:, c0:c0 + FF_COLS]
        return (cb_ref[:, c0:c0 + FF_COLS]
                + w[2:3] * scr[HALO:HALO + tm]
                + w[1:2] * scr[HALO - 1:HALO - 1 + tm]
                + w[0:1] * scr[HALO - 2:HALO - 2 + tm])

    for j in range(D_FF // FF_COLS):
        ca, cb = j * FF_COLS, D_FF + j * FF_COLS
        ua_scr[...] = _dot(he, wup_ref[:, ca:ca + FF_COLS])
        ub_scr[...] = _dot(he, wup_ref[:, cb:cb + FF_COLS])
        act = (jax.nn.silu(conv(ua_scr, ca)) * conv(ub_scr, cb)).astype(BF16)
        part = _dot(act, wdn_ref[ca:ca + FF_COLS, :])
        if j == 0:
            acc_scr[...] = part
        else:
            acc_scr[...] += part
    y = x + acc_scr[...]
    if final_norm:
        y = _rms(y, gfin_ref[...])
    o_ref[...] = y


def _ffn(x2, g, wup, cw, cb, wdn, gfin, seq, final_norm):
    m = x2.shape[0]
    tm = ROW_TILE
    assert seq % tm == 0 and D_FF % FF_COLS == 0 and tm % HALO == 0
    row = pl.BlockSpec((tm, D_MODEL), lambda i: (i, 0))
    halo = pl.BlockSpec((HALO, D_MODEL), lambda i: (jnp.maximum(i * (tm // HALO) - 1, 0), 0))
    return pl.pallas_call(
        functools.partial(_ffn_kernel, tiles_per_seq=seq // tm, final_norm=final_norm),
        grid=(m // tm,),
        in_specs=[row, halo, _resident(g.shape), _resident(wup.shape), _resident(cw.shape),
                  _resident(cb.shape), _resident(wdn.shape), _resident(gfin.shape)],
        out_specs=row,
        out_shape=jax.ShapeDtypeStruct((m, D_MODEL), F32),
        scratch_shapes=[pltpu.VMEM((tm + HALO, FF_COLS), F32),
                        pltpu.VMEM((tm + HALO, FF_COLS), F32),
                        pltpu.VMEM((tm, D_MODEL), F32)],
        compiler_params=_params(("parallel",)),
        name="conv_ffn",
    )(x2, x2, g, wup, cw, cb, wdn, gfin)


def _hgrn_kernel(x_ref, g_ref, w_ref, lbp_ref, gout_ref, o_ref, proj_scr, st_scr, *, layer):
    ts = x_ref.shape[0]
    kw = C_HEADS * C_KEY_DIM
    c = CHUNK
    nsub = c // HG_SUB

    @pl.when(pl.program_id(1) == 0)
    def _():
        st_scr[...] = jnp.zeros(st_scr.shape, F32)

    h = _rms(x_ref[...], g_ref[...]).astype(BF16)
    step = 512
    for n in range(proj_scr.shape[1] // step):
        proj_scr[:, n * step:(n + 1) * step] = _dot(h, w_ref[:, n * step:(n + 1) * step])

    lbp = lbp_ref[...]
    e = jnp.exp(lbp - jnp.max(lbp, axis=0, keepdims=True))
    sm = e / jnp.sum(e, axis=0, keepdims=True)
    lb = jnp.sum(sm[1:layer + 1], axis=0, keepdims=True) if layer > 0 else jnp.zeros((1, kw), F32)

    ri = lax.broadcasted_iota(I32, (c, c), 0)
    ci = lax.broadcasted_iota(I32, (c, c), 1)
    r3 = lax.broadcasted_iota(I32, (c, 3 * c), 0)
    c3 = lax.broadcasted_iota(I32, (c, 3 * c), 1) % c
    tri3 = jnp.where(r3 >= c3, 1.0, 0.0).astype(BF16)
    dmask = (ri >= ci) & (ri // HG_SUB == ci // HG_SUB)

    def chunk_body(ic, carry):
        rows = pl.ds(pl.multiple_of(ic * c, c), c)
        gg = lb + (1.0 - lb) * jax.nn.sigmoid(proj_scr[rows, kw:2 * kw])
        lg = jnp.log(gg)
        hi = lg.astype(BF16)
        r1 = lg - hi.astype(F32)
        mid = r1.astype(BF16)
        lo = (r1 - mid.astype(F32)).astype(BF16)
        b_all = _dot(tri3, jnp.concatenate([hi, mid, lo], axis=0))
        for hd in range(C_HEADS):
            kc = slice(hd * C_KEY_DIM, (hd + 1) * C_KEY_DIM)
            vc = slice(hd * C_VAL_DIM, (hd + 1) * C_VAL_DIM)
            q = jax.nn.silu(proj_scr[rows, hd * C_KEY_DIM:(hd + 1) * C_KEY_DIM])
            kk = 1.0 - gg[:, kc]
            b = b_all[:, kc]
            iv = proj_scr[rows, 2 * kw + hd * C_VAL_DIM:2 * kw + (hd + 1) * C_VAL_DIM].astype(BF16)
            gate = proj_scr[rows, 2 * kw + C_WIDTH + hd * C_VAL_DIM:
                            2 * kw + C_WIDTH + (hd + 1) * C_VAL_DIM]
            st = st_scr[hd]
            inter = _dot_nt((q * jnp.exp(b)).astype(BF16), st.astype(BF16))
            qs, ks = [], []
            for j in range(nsub - 1):
                e0, e1 = j * HG_SUB, (j + 1) * HG_SUB
                rj = b[e1 - 1:e1]
                qs.append(jnp.concatenate(
                    [jnp.zeros((e1, C_KEY_DIM), F32), q[e1:] * jnp.exp(b[e1:] - rj)], axis=0))
                kj = kk[e0:e1] * jnp.exp(rj - b[e0:e1])
                parts = [kj, jnp.zeros((c - e1, C_KEY_DIM), F32)]
                if e0:
                    parts = [jnp.zeros((e0, C_KEY_DIM), F32)] + parts
                ks.append(jnp.concatenate(parts, axis=0))
            a_off = _dot_nt(jnp.concatenate(qs, axis=1).astype(BF16),
                            jnp.concatenate(ks, axis=1).astype(BF16))
            bm = jnp.concatenate(
                [jnp.broadcast_to(b[j * HG_SUB + HG_SUB // 2:j * HG_SUB + HG_SUB // 2 + 1],
                                  (HG_SUB, C_KEY_DIM)) for j in range(nsub)], axis=0)
            a_dg = _dot_nt((q * jnp.exp(b - bm)).astype(BF16), (kk * jnp.exp(bm - b)).astype(BF16))
            a = a_off + jnp.where(dmask, a_dg, 0.0)
            o = inter + _dot(a.astype(BF16), iv)
            bl = b[c - 1:c]
            upd = _dot_tn(iv, (kk * jnp.exp(bl - b)).astype(BF16))
            st_scr[hd] = st * jnp.exp(bl) + upd
            on = _rms(o, gout_ref[...])
            o_ref[rows, vc] = (on * jax.nn.silu(gate)).astype(BF16)
        return carry

    lax.fori_loop(0, ts // c, chunk_body, 0)


def _hgrn(x2, g, w, lbp, gout, bsz, seq, layer):
    ts = HG_TILE
    nt = seq // ts
    assert seq % ts == 0 and ts % CHUNK == 0 and CHUNK % HG_SUB == 0
    row = lambda n: pl.BlockSpec((ts, n), lambda b, i: (b * nt + i, 0))
    return pl.pallas_call(
        functools.partial(_hgrn_kernel, layer=layer),
        grid=(bsz, nt),
        in_specs=[row(D_MODEL), _resident(g.shape), _resident(w.shape), _resident(lbp.shape),
                  _resident(gout.shape)],
        out_specs=row(C_WIDTH),
        out_shape=jax.ShapeDtypeStruct((bsz * seq, C_WIDTH), BF16),
        scratch_shapes=[pltpu.VMEM((ts, w.shape[1]), F32),
                        pltpu.VMEM((C_HEADS, C_VAL_DIM, C_KEY_DIM), F32)],
        compiler_params=_params(("parallel", "arbitrary")),
        name="hgrn2",
    )(x2, g, w, lbp, gout)


def kernel(x, rel_bias, hgrn_lb, mix_norm, ffn_norm, final_norm, ab_w_in, ab_idx_k_norm, ab_gmlp_norm, ab_w_s, ab_b_s, ab_w_out, c_w_in, c_out_norm, c_w_out, ffn_w_up, ffn_conv_w, ffn_conv_b, ffn_w_down):
    bsz, seq, d = x.shape
    depth = mix_norm.shape[0]
    x2 = x.reshape(bsz * seq, d)
    row = lambda a: a.reshape(1, -1)
    bias = _bias_tiles(rel_bias)
    for l in range(depth):
        if l % 2 == 0:
            e = l // 2
            w = ab_w_in[e]
            offs = [0]
            for s in (A_WIDTH, A_WIDTH, A_WIDTH, IDX_HEADS * IDX_DIM, IDX_DIM, IDX_HEADS,
                      B_WIDTH, B_WIDTH):
                offs.append(offs[-1] + s)
            seg = [w[:, offs[n]:offs[n + 1]] for n in range(8)]
            w_p = jnp.concatenate(
                seg[0:4] + seg[6:8] + [seg[4], seg[4], seg[5],
                                       jnp.zeros((d, LANES - IDX_HEADS), w.dtype)],
                axis=1).astype(BF16)
            assert w_p.shape[1] == _C_END
            gik = jnp.concatenate([ab_idx_k_norm[e], ab_idx_k_norm[e]])
            q, k, v, iq, ik2, iw, y_b = _proj_ab(
                x2, row(mix_norm[l]), w_p, row(gik), row(ab_gmlp_norm[e]),
                ab_w_s[e], jnp.transpose(ab_b_s[e]))
            vt = jnp.transpose(v.reshape(bsz, seq, A_WIDTH), (0, 2, 1)).reshape(bsz * A_WIDTH, seq)
            y_a = _dsa(q, k, vt, iq, ik2, jnp.transpose(iw), bias, bsz, seq)
            x2 = _outproj(x2, [y_a, y_b], ab_w_out[e].astype(BF16))
        else:
            o_i = l // 2
            og = _hgrn(x2, row(mix_norm[l]), c_w_in[o_i].astype(BF16), hgrn_lb,
                       row(c_out_norm[o_i]), bsz, seq, l)
            x2 = _outproj(x2, [og], c_w_out[o_i].astype(BF16))
        x2 = _ffn(x2, row(ffn_norm[l]), ffn_w_up[l].astype(BF16), ffn_conv_w[l],
                  row(ffn_conv_b[l]), ffn_w_down[l].astype(BF16), row(final_norm),
                  seq, l == depth - 1)
    return x2.reshape(bsz, seq, d)
```

```python
import functools
import math

import jax
import jax.numpy as jnp
from jax import lax
from jax.experimental import pallas as pl
from jax.experimental.pallas import tpu as pltpu

F32 = jnp.float32
BF16 = jnp.bfloat16
I32 = jnp.int32

D_MODEL = 1024
CHUNK = 64
A_HEADS = 8
A_HEAD_DIM = 64
A_WIDTH = A_HEADS * A_HEAD_DIM
IDX_HEADS = 16
IDX_DIM = 64
TOPK_MAX = 256
REL_BUCKETS = 32
REL_MAX_DIST = 128
B_GROUPS = 8
B_GROUP_DIM = 64
B_WIDTH = B_GROUPS * B_GROUP_DIM
B_CHUNK = 128
C_HEADS = 8
C_KEY_DIM = 128
C_VAL_DIM = 128
C_WIDTH = C_HEADS * C_VAL_DIM
D_FF = 2816
EPS = 1e-6

LANES = 128
VMEM_LIMIT = 56 * 1024 * 1024

ROW_TILE = 512
DSA_TILE = 256
HG_TILE = 256
HG_SUB = 16
FF_COLS = 256
NEG = -1e30
M_INIT = -1e29
INT_MIN = -2 ** 31
SUM_ROWS = 16

_C_Q, _C_K, _C_V, _C_IQ, _C_U, _C_VG, _C_IK, _C_IW, _C_END = (
    0, 512, 1024, 1536, 2560, 3072, 3584, 3712, 3840)


def _params(sem):
    return pltpu.CompilerParams(dimension_semantics=sem, vmem_limit_bytes=VMEM_LIMIT)


def _resident(shape):
    nd = len(shape)
    return pl.BlockSpec(shape, lambda *_: (0,) * nd, pipeline_mode=pl.Buffered(1))


def _rms(x, g):
    ms = jnp.mean(x * x, axis=-1, keepdims=True)
    return x * lax.rsqrt(ms + EPS) * g


def _dot(a, b):
    return jnp.dot(a, b, preferred_element_type=F32)


def _dot_nt(a, b):
    return lax.dot_general(a, b, (((1,), (1,)), ((), ())), preferred_element_type=F32)


def _dot_tn(a, b):
    return lax.dot_general(a, b, (((0,), (0,)), ((), ())), preferred_element_type=F32)


def _t5_bucket(rel):
    half = REL_BUCKETS // 2
    max_exact = half // 2
    ret = jnp.where(rel > 0, half, 0)
    n = jnp.abs(rel)
    nf = jnp.maximum(n, max_exact).astype(F32)
    large = max_exact + (jnp.log(nf / max_exact) / math.log(REL_MAX_DIST / max_exact)
                         * (half - max_exact)).astype(I32)
    large = jnp.minimum(large, half - 1)
    return ret + jnp.where(n < max_exact, n, large)


def _bias_kernel(rb_ref, bucket_ref, far_ref, o_ref):
    h = pl.program_id(1)

    def lookup(bucket):
        tile = jnp.zeros(bucket.shape, F32)
        for b in range(REL_BUCKETS):
            tile = jnp.where(bucket == b, rb_ref[b, h], tile)
        return tile

    o_ref[0, 0] = lookup(bucket_ref[0]) - lookup(far_ref[0])


def _bias_tiles(rel_bias):
    t = DSA_TILE
    assert t >= REL_MAX_DIST
    kk = jnp.arange(t, dtype=I32)[:, None]
    qq = jnp.arange(t, dtype=I32)[None, :]
    rel = jnp.stack([kk - qq, kk - qq - t, jnp.full((t, t), -2 * t, I32)])
    bucket = _t5_bucket(rel)
    return pl.pallas_call(
        _bias_kernel,
        grid=(2, A_HEADS),
        in_specs=[pl.BlockSpec(memory_space=pltpu.SMEM),
                  pl.BlockSpec((1, t, t), lambda d, h: (d, 0, 0)),
                  pl.BlockSpec((1, t, t), lambda d, h: (2, 0, 0))],
        out_specs=pl.BlockSpec((1, 1, t, t), lambda d, h: (d, h, 0, 0)),
        out_shape=jax.ShapeDtypeStruct((2, A_HEADS, t, t), F32),
        compiler_params=_params(("arbitrary", "arbitrary")),
        name="bias_tiles",
    )(rel_bias, bucket, bucket)


def _proj_ab_kernel(x_ref, g_ref, w_ref, gik_ref, ggm_ref, ws_ref, bs_ref,
                    q_ref, k_ref, v_ref, iq_ref, ik_ref, iw_ref, yb_ref):
    tm = x_ref.shape[0]
    h = _rms(x_ref[...], g_ref[...]).astype(BF16)

    def proj(c0, c1):
        return _dot(h, w_ref[:, c0:c1])

    q_ref[...] = (proj(_C_Q, _C_K) * (A_HEAD_DIM ** -0.5)).astype(BF16)
    k_ref[...] = proj(_C_K, _C_V).astype(BF16)
    v_ref[...] = proj(_C_V, _C_IQ).astype(BF16)
    half = (_C_U - _C_IQ) // 2
    for j in range(2):
        iq_ref[:, j * half:(j + 1) * half] = (
            proj(_C_IQ + j * half, _C_IQ + (j + 1) * half) * (IDX_DIM ** -0.5)).astype(BF16)
    ik_ref[...] = _rms(proj(_C_IK, _C_IW), gik_ref[...]).astype(BF16)
    iw_ref[...] = proj(_C_IW, _C_END)[:, :IDX_HEADS] * (IDX_HEADS ** -0.5)

    u = jax.nn.gelu(proj(_C_U, _C_VG))
    vn = _rms(jax.nn.gelu(proj(_C_VG, _C_IK)), ggm_ref[...]).astype(BF16)
    ri = lax.broadcasted_iota(I32, (B_CHUNK, B_CHUNK), 0) // CHUNK
    ci = lax.broadcasted_iota(I32, (B_CHUNK, B_CHUNK), 1) // CHUNK
    low = ri >= ci
    first = lax.broadcasted_iota(I32, (B_CHUNK, LANES), 1) < B_GROUP_DIM
    for p in range(B_GROUPS // 2):
        w_e = jnp.where(low, ws_ref[2 * p], 0.0).astype(BF16)
        w_o = jnp.where(low, ws_ref[2 * p + 1], 0.0).astype(BF16)
        b_e = bs_ref[:, 2 * p:2 * p + 1]
        b_o = bs_ref[:, 2 * p + 1:2 * p + 2]
        cols = slice(p * LANES, (p + 1) * LANES)
        for r in range(tm // B_CHUNK):
            rows = slice(r * B_CHUNK, (r + 1) * B_CHUNK)
            vp = vn[rows, cols]
            s = jnp.where(first, _dot(w_e, vp) + b_e, _dot(w_o, vp) + b_o)
            yb_ref[rows, cols] = (u[rows, cols] * s).astype(BF16)


def _proj_ab(x2, g, w, gik, ggm, ws, bs_t):
    m = x2.shape[0]
    tm = ROW_TILE
    row = lambda n: pl.BlockSpec((tm, n), lambda i: (i, 0))
    outs = [(A_WIDTH, BF16), (A_WIDTH, BF16), (A_WIDTH, BF16), (IDX_HEADS * IDX_DIM, BF16),
            (2 * IDX_DIM, BF16), (IDX_HEADS, F32), (B_WIDTH, BF16)]
    return pl.pallas_call(
        _proj_ab_kernel,
        grid=(m // tm,),
        in_specs=[row(D_MODEL), _resident(g.shape), _resident(w.shape), _resident(gik.shape),
                  _resident(ggm.shape), _resident(ws.shape), _resident(bs_t.shape)],
        out_specs=[row(n) for n, _ in outs],
        out_shape=[jax.ShapeDtypeStruct((m, n), dt) for n, dt in outs],
        compiler_params=_params(("parallel",)),
        name="proj_ab",
    )(x2, g, w, gik, ggm, ws, bs_t)


def _dsa_kernel(q_ref, k_ref, vt_ref, iq_ref, ik_ref, iwt_ref, bias_ref, o_ref,
                keys_scr, madd_scr, iqm_scr, qm_scr, ot_scr, m_scr, alpha_scr, acc_scr, s_scr,
                *, top_k):
    t = DSA_TILE
    sub = 8
    i = pl.program_id(1)
    first = lax.broadcasted_iota(I32, (t, LANES), 1) < A_HEAD_DIM
    allowed = (lax.broadcasted_iota(I32, (t, t), 0) // CHUNK
               <= lax.broadcasted_iota(I32, (t, t), 1) // CHUNK)

    def blk(kb):
        return pl.ds(pl.multiple_of(kb * t, t), t)

    for h in range(IDX_HEADS):
        pair = iq_ref[:, (h // 2) * LANES:(h // 2 + 1) * LANES]
        iqm_scr[h] = jnp.where(first if h % 2 == 0 else ~first, pair, jnp.zeros_like(pair))
    for h in range(A_HEADS):
        pair = q_ref[:, (h // 2) * LANES:(h // 2 + 1) * LANES]
        qm_scr[h] = jnp.where(first if h % 2 == 0 else ~first, pair, jnp.zeros_like(pair))

    def score_keys(kb, diag):
        ikb = ik_ref[blk(kb), :]
        acc = jnp.zeros((t, t), F32)
        for h in range(IDX_HEADS):
            s = _dot_nt(ikb, iqm_scr[h])
            acc = acc + jnp.maximum(s, 0.0) * iwt_ref[h:h + 1, :]
        bits = lax.bitcast_convert_type(acc, I32)
        key = bits ^ ((bits >> 31) & jnp.int32(0x7FFFFFFF))
        if diag:
            key = jnp.where(allowed, key, jnp.int32(INT_MIN))
        keys_scr[blk(kb), :] = key

    def score_body(kb, c):
        score_keys(kb, False)
        return c

    lax.fori_loop(0, i, score_body, 0)
    score_keys(i, True)

    @pl.when(i == 0)
    def _():
        madd_scr[0:t, :] = jnp.where(allowed, 0.0, NEG)

    @pl.when(i > 0)
    def _():
        def count(cmp, thr):
            def body(kb, cnt):
                kk = keys_scr[blk(kb), :]
                for r in range(t // sub):
                    cnt = cnt + jnp.where(cmp(kk[r * sub:(r + 1) * sub], thr), 1.0, 0.0)
                return cnt
            cnt = lax.fori_loop(0, i + 1, body, jnp.zeros((sub, t), F32))
            return jnp.sum(cnt, axis=0, keepdims=True)

        ge = lambda a, b: a >= b
        gt = lambda a, b: a > b

        def bit_body(it, thr):
            cand = thr + (jnp.int32(1) << (31 - it))
            return jnp.where(count(ge, cand) >= top_k, cand, thr)

        thr = lax.fori_loop(0, 32, bit_body, jnp.full((sub, t), INT_MIN, I32))
        n_ge = count(ge, thr)
        thr_row = thr[0:1]

        def plain_body(kb, c):
            madd_scr[blk(kb), :] = jnp.where(keys_scr[blk(kb), :] >= thr_row, 0.0, NEG)
            return c

        lax.fori_loop(0, i + 1, plain_body, 0)

        @pl.when(jnp.max(n_ge) > top_k)
        def _():
            need = top_k - count(gt, thr)
            lower = (lax.broadcasted_iota(I32, (t, t), 1)
                     < lax.broadcasted_iota(I32, (t, t), 0))
            lower = jnp.where(lower, 1.0, 0.0).astype(BF16)

            def tie_body(kb, run):
                kk = keys_scr[blk(kb), :]
                eq = jnp.where(kk == thr_row, 1.0, 0.0)
                rank = run + _dot(lower, eq.astype(BF16))
                sel = (kk > thr_row) | ((kk == thr_row) & (rank < need))
                madd_scr[blk(kb), :] = jnp.where(sel, 0.0, NEG)
                return run + jnp.sum(eq, axis=0, keepdims=True)

            lax.fori_loop(0, i + 1, tie_body, jnp.zeros((1, t), F32))

    ones = jnp.ones((SUM_ROWS, t), BF16)

    m_scr[...] = jnp.full(m_scr.shape, M_INIT, F32)
    acc_scr[...] = jnp.zeros(acc_scr.shape, F32)

    def att_block(kb, didx):
        @pl.when(kb <= i)
        def _():
            madd = madd_scr[blk(kb), :]
            for h in range(A_HEADS):
                cols = slice((h // 2) * LANES, (h // 2 + 1) * LANES)
                s = _dot_nt(k_ref[blk(kb), cols], qm_scr[h]) + madd
                if didx is not None:
                    s = s + bias_ref[didx, h]
                s_scr[h] = s
                m = m_scr[h]
                m_new = jnp.maximum(m, jnp.max(s, axis=0, keepdims=True))
                alpha_scr[h] = jnp.exp(m - m_new)
                m_scr[h] = m_new

        for h in range(A_HEADS):
            pr = jnp.exp(s_scr[h] - m_scr[h]).astype(BF16)
            v1 = jnp.concatenate(
                [vt_ref[h * A_HEAD_DIM:(h + 1) * A_HEAD_DIM, blk(kb)], ones], axis=0)
            acc_scr[h] = alpha_scr[h] * acc_scr[h] + _dot(v1, pr)

    def att_body(kb, c, didx):
        att_block(kb, didx)
        return c

    n_far = jnp.maximum(i - 1, 0)
    lax.fori_loop(0, n_far, functools.partial(att_body, didx=None), 0)
    lax.fori_loop(n_far, i, functools.partial(att_body, didx=1), 0)
    att_block(i, 0)
    for h in range(A_HEADS):
        acc = acc_scr[h]
        ot_scr[h * A_HEAD_DIM:(h + 1) * A_HEAD_DIM, :] = (
            acc[:A_HEAD_DIM] * (1.0 / acc[A_HEAD_DIM:A_HEAD_DIM + 1]))
    o_ref[...] = jnp.transpose(ot_scr[...]).astype(BF16)


def _dsa(q, k, vt, iq, ik2, iwt, bias, bsz, seq):
    t = DSA_TILE
    nq = seq // t
    top_k = min(TOPK_MAX, seq // 4)
    assert t <= top_k and t % CHUNK == 0 and seq % t == 0
    qrow = lambda n: pl.BlockSpec((t, n), lambda b, i: (b * nq + i, 0))
    full = lambda n: pl.BlockSpec((seq, n), lambda b, i: (b, 0))
    return pl.pallas_call(
        functools.partial(_dsa_kernel, top_k=top_k),
        grid=(bsz, nq),
        in_specs=[qrow(A_WIDTH), full(A_WIDTH),
                  pl.BlockSpec((A_WIDTH, seq), lambda b, i: (b, 0)),
                  qrow(IDX_HEADS * IDX_DIM), full(2 * IDX_DIM),
                  pl.BlockSpec((IDX_HEADS, t), lambda b, i: (0, b * nq + i)),
                  _resident(bias.shape)],
        out_specs=qrow(A_WIDTH),
        out_shape=jax.ShapeDtypeStruct((bsz * seq, A_WIDTH), BF16),
        scratch_shapes=[pltpu.VMEM((seq, t), I32), pltpu.VMEM((seq, t), F32),
                        pltpu.VMEM((IDX_HEADS, t, LANES), BF16),
                        pltpu.VMEM((A_HEADS, t, LANES), BF16),
                        pltpu.VMEM((A_WIDTH, t), F32),
                        pltpu.VMEM((A_HEADS, 1, t), F32),
                        pltpu.VMEM((A_HEADS, 1, t), F32),
                        pltpu.VMEM((A_HEADS, A_HEAD_DIM + SUM_ROWS, t), F32),
                        pltpu.VMEM((A_HEADS, t, t), F32)],
        compiler_params=_params(("parallel", "arbitrary")),
        name="dsa",
    )(q, k, vt, iq, ik2, iwt, bias)


HALO = 16


def _ffn_kernel(*refs, n_y, tiles_per_seq, final_norm):
    x_ref, xh_ref = refs[0], refs[1]
    y_refs = refs[2:2 + 2 * n_y]
    (wo_ref, g_ref, wup_ref, cw_ref, cb_ref, wdn_ref, gfin_ref, o_ref,
     ua_scr, ub_scr, acc_scr) = refs[2 + 2 * n_y:]
    tm = x_ref.shape[0]

    def mixed(x_r, ys):
        acc = x_r[...]
        r0 = 0
        for y_r in ys:
            n = y_r.shape[1]
            acc = acc + _dot(y_r[...], wo_ref[r0:r0 + n, :])
            r0 += n
        return acc

    x = mixed(x_ref, y_refs[0::2])
    hn = _rms(x, g_ref[...])
    keep = jnp.where(pl.program_id(0) % tiles_per_seq == 0, 0.0, 1.0)
    hh = _rms(mixed(xh_ref, y_refs[1::2]), g_ref[...]) * keep
    he = jnp.concatenate([hh, hn], axis=0).astype(BF16)

    def conv(scr, c0):
        w = cw_ref[:, c0:c0 + FF_COLS]
        return (cb_ref[:, c0:c0 + FF_COLS]
                + w[2:3] * scr[HALO:HALO + tm]
                + w[1:2] * scr[HALO - 1:HALO - 1 + tm]
                + w[0:1] * scr[HALO - 2:HALO - 2 + tm])

    nj = D_FF // FF_COLS

    def up(j):
        ua_scr[j % 2] = _dot(he, wup_ref[:, j * FF_COLS:(j + 1) * FF_COLS])
        ub_scr[j % 2] = _dot(he, wup_ref[:, D_FF + j * FF_COLS:D_FF + (j + 1) * FF_COLS])

    up(0)
    for j in range(nj):
        ca, cb = j * FF_COLS, D_FF + j * FF_COLS
        if j + 1 < nj:
            up(j + 1)
        act = (jax.nn.silu(conv(ua_scr.at[j % 2], ca)) * conv(ub_scr.at[j % 2], cb)).astype(BF16)
        part = _dot(act, wdn_ref[ca:ca + FF_COLS, :])
        if j == 0:
            acc_scr[...] = part
        else:
            acc_scr[...] += part
    y = x + acc_scr[...]
    if final_norm:
        y = _rms(y, gfin_ref[...])
    o_ref[...] = y


def _ffn(x2, ys, wo, g, wup, cw, cb, wdn, gfin, seq, final_norm):
    m = x2.shape[0]
    tm = ROW_TILE
    assert seq % tm == 0 and D_FF % FF_COLS == 0 and tm % HALO == 0
    row = lambda n: pl.BlockSpec((tm, n), lambda i: (i, 0))
    halo = lambda n: pl.BlockSpec((HALO, n), lambda i: (jnp.maximum(i * (tm // HALO) - 1, 0), 0))
    y_specs, y_args = [], []
    for y in ys:
        y_specs += [row(y.shape[1]), halo(y.shape[1])]
        y_args += [y, y]
    consts = (wo, g, wup, cw, cb, wdn, gfin)
    return pl.pallas_call(
        functools.partial(_ffn_kernel, n_y=len(ys), tiles_per_seq=seq // tm,
                          final_norm=final_norm),
        grid=(m // tm,),
        in_specs=[row(D_MODEL), halo(D_MODEL)] + y_specs + [_resident(c.shape) for c in consts],
        out_specs=row(D_MODEL),
        out_shape=jax.ShapeDtypeStruct((m, D_MODEL), F32),
        scratch_shapes=[pltpu.VMEM((2, tm + HALO, FF_COLS), F32),
                        pltpu.VMEM((2, tm + HALO, FF_COLS), F32),
                        pltpu.VMEM((tm, D_MODEL), F32)],
        compiler_params=_params(("parallel",)),
        name="conv_ffn",
    )(x2, x2, *y_args, *consts)


def _hgrn_kernel(x_ref, g_ref, w_ref, lbp_ref, gout_ref, o_ref, proj_scr, st_scr, *, layer):
    ts = x_ref.shape[0]
    kw = C_HEADS * C_KEY_DIM
    c = CHUNK
    nsub = c // HG_SUB

    @pl.when(pl.program_id(1) == 0)
    def _():
        st_scr[...] = jnp.zeros(st_scr.shape, F32)

    h = _rms(x_ref[...], g_ref[...]).astype(BF16)
    step = 512
    for n in range(proj_scr.shape[1] // step):
        proj_scr[:, n * step:(n + 1) * step] = _dot(h, w_ref[:, n * step:(n + 1) * step])

    lbp = lbp_ref[...]
    e = jnp.exp(lbp - jnp.max(lbp, axis=0, keepdims=True))
    sm = e / jnp.sum(e, axis=0, keepdims=True)
    lb = jnp.sum(sm[1:layer + 1], axis=0, keepdims=True) if layer > 0 else jnp.zeros((1, kw), F32)

    ri = lax.broadcasted_iota(I32, (c, c), 0)
    ci = lax.broadcasted_iota(I32, (c, c), 1)
    r3 = lax.broadcasted_iota(I32, (c, 3 * c), 0)
    c3 = lax.broadcasted_iota(I32, (c, 3 * c), 1) % c
    tri3 = jnp.where(r3 >= c3, 1.0, 0.0).astype(BF16)
    dmask = (ri >= ci) & (ri // HG_SUB == ci // HG_SUB)

    def chunk_body(ic, carry):
        rows = pl.ds(pl.multiple_of(ic * c, c), c)
        gg = lb + (1.0 - lb) * jax.nn.sigmoid(proj_scr[rows, kw:2 * kw])
        lg = jnp.log(gg)
        hi = lg.astype(BF16)
        r1 = lg - hi.astype(F32)
        mid = r1.astype(BF16)
        lo = (r1 - mid.astype(F32)).astype(BF16)
        b_all = _dot(tri3, jnp.concatenate([hi, mid, lo], axis=0))
        for hd in range(C_HEADS):
            kc = slice(hd * C_KEY_DIM, (hd + 1) * C_KEY_DIM)
            vc = slice(hd * C_VAL_DIM, (hd + 1) * C_VAL_DIM)
            q = jax.nn.silu(proj_scr[rows, hd * C_KEY_DIM:(hd + 1) * C_KEY_DIM])
            kk = 1.0 - gg[:, kc]
            b = b_all[:, kc]
            iv = proj_scr[rows, 2 * kw + hd * C_VAL_DIM:2 * kw + (hd + 1) * C_VAL_DIM].astype(BF16)
            gate = proj_scr[rows, 2 * kw + C_WIDTH + hd * C_VAL_DIM:
                            2 * kw + C_WIDTH + (hd + 1) * C_VAL_DIM]
            st = st_scr[hd]
            inter = _dot_nt((q * jnp.exp(b)).astype(BF16), st.astype(BF16))
            qs, ks = [], []
            for j in range(nsub - 1):
                e0, e1 = j * HG_SUB, (j + 1) * HG_SUB
                rj = b[e1 - 1:e1]
                qs.append(jnp.concatenate(
                    [jnp.zeros((e1, C_KEY_DIM), F32), q[e1:] * jnp.exp(b[e1:] - rj)], axis=0))
                kj = kk[e0:e1] * jnp.exp(rj - b[e0:e1])
                parts = [kj, jnp.zeros((c - e1, C_KEY_DIM), F32)]
                if e0:
                    parts = [jnp.zeros((e0, C_KEY_DIM), F32)] + parts
                ks.append(jnp.concatenate(parts, axis=0))
            a_off = _dot_nt(jnp.concatenate(qs, axis=1).astype(BF16),
                            jnp.concatenate(ks, axis=1).astype(BF16))
            bm = jnp.concatenate(
                [jnp.broadcast_to(b[j * HG_SUB + HG_SUB // 2:j * HG_SUB + HG_SUB // 2 + 1],
                                  (HG_SUB, C_KEY_DIM)) for j in range(nsub)], axis=0)
            a_dg = _dot_nt((q * jnp.exp(b - bm)).astype(BF16), (kk * jnp.exp(bm - b)).astype(BF16))
            a = a_off + jnp.where(dmask, a_dg, 0.0)
            o = inter + _dot(a.astype(BF16), iv)
            bl = b[c - 1:c]
            upd = _dot_tn(iv, (kk * jnp.exp(bl - b)).astype(BF16))
            st_scr[hd] = st * jnp.exp(bl) + upd
            on = _rms(o, gout_ref[...])
            o_ref[rows, vc] = (on * jax.nn.silu(gate)).astype(BF16)
        return carry

    lax.fori_loop(0, ts // c, chunk_body, 0)


def _hgrn(x2, g, w, lbp, gout, bsz, seq, layer):
    ts = HG_TILE
    nt = seq // ts
    assert seq % ts == 0 and ts % CHUNK == 0 and CHUNK % HG_SUB == 0
    row = lambda n: pl.BlockSpec((ts, n), lambda b, i: (b * nt + i, 0))
    return pl.pallas_call(
        functools.partial(_hgrn_kernel, layer=layer),
        grid=(bsz, nt),
        in_specs=[row(D_MODEL), _resident(g.shape), _resident(w.shape), _resident(lbp.shape),
                  _resident(gout.shape)],
        out_specs=row(C_WIDTH),
        out_shape=jax.ShapeDtypeStruct((bsz * seq, C_WIDTH), BF16),
        scratch_shapes=[pltpu.VMEM((ts, w.shape[1]), F32),
                        pltpu.VMEM((C_HEADS, C_VAL_DIM, C_KEY_DIM), F32)],
        compiler_params=_params(("parallel", "arbitrary")),
        name="hgrn2",
    )(x2, g, w, lbp, gout)


def kernel(x, rel_bias, hgrn_lb, mix_norm, ffn_norm, final_norm, ab_w_in, ab_idx_k_norm, ab_gmlp_norm, ab_w_s, ab_b_s, ab_w_out, c_w_in, c_out_norm, c_w_out, ffn_w_up, ffn_conv_w, ffn_conv_b, ffn_w_down):
    bsz, seq, d = x.shape
    depth = mix_norm.shape[0]
    x2 = x.reshape(bsz * seq, d)
    row = lambda a: a.reshape(1, -1)
    bias = _bias_tiles(rel_bias)
    for l in range(depth):
        if l % 2 == 0:
            e = l // 2
            w = ab_w_in[e]
            offs = [0]
            for s in (A_WIDTH, A_WIDTH, A_WIDTH, IDX_HEADS * IDX_DIM, IDX_DIM, IDX_HEADS,
                      B_WIDTH, B_WIDTH):
                offs.append(offs[-1] + s)
            seg = [w[:, offs[n]:offs[n + 1]] for n in range(8)]
            w_p = jnp.concatenate(
                seg[0:4] + seg[6:8] + [seg[4], seg[4], seg[5],
                                       jnp.zeros((d, LANES - IDX_HEADS), w.dtype)],
                axis=1).astype(BF16)
            assert w_p.shape[1] == _C_END
            gik = jnp.concatenate([ab_idx_k_norm[e], ab_idx_k_norm[e]])
            q, k, v, iq, ik2, iw, y_b = _proj_ab(
                x2, row(mix_norm[l]), w_p, row(gik), row(ab_gmlp_norm[e]),
                ab_w_s[e], jnp.transpose(ab_b_s[e]))
            vt = jnp.transpose(v.reshape(bsz, seq, A_WIDTH), (0, 2, 1)).reshape(bsz * A_WIDTH, seq)
            y_a = _dsa(q, k, vt, iq, ik2, jnp.transpose(iw), bias, bsz, seq)
            ys, wo = [y_a, y_b], ab_w_out[e]
        else:
            o_i = l // 2
            og = _hgrn(x2, row(mix_norm[l]), c_w_in[o_i].astype(BF16), hgrn_lb,
                       row(c_out_norm[o_i]), bsz, seq, l)
            ys, wo = [og], c_w_out[o_i]
        x2 = _ffn(x2, ys, wo.astype(BF16), row(ffn_norm[l]), ffn_w_up[l].astype(BF16),
                  ffn_conv_w[l], row(ffn_conv_b[l]), ffn_w_down[l].astype(BF16),
                  row(final_norm), seq, l == depth - 1)
    return x2.reshape(bsz, seq, d)
```

```python
import functools
import math

import jax
import jax.numpy as jnp
from jax import lax
from jax.experimental import pallas as pl
from jax.experimental.pallas import tpu as pltpu

F32 = jnp.float32
BF16 = jnp.bfloat16
I32 = jnp.int32

D_MODEL = 1024
CHUNK = 64
A_HEADS = 8
A_HEAD_DIM = 64
A_WIDTH = A_HEADS * A_HEAD_DIM
IDX_HEADS = 16
IDX_DIM = 64
TOPK_MAX = 256
REL_BUCKETS = 32
REL_MAX_DIST = 128
B_GROUPS = 8
B_GROUP_DIM = 64
B_WIDTH = B_GROUPS * B_GROUP_DIM
B_CHUNK = 128
C_HEADS = 8
C_KEY_DIM = 128
C_VAL_DIM = 128
C_WIDTH = C_HEADS * C_VAL_DIM
D_FF = 2816
EPS = 1e-6

LANES = 128
VMEM_LIMIT = 56 * 1024 * 1024

ROW_TILE = 512
DSA_TILE = 256
HG_TILE = 256
HG_SUB = 16
FF_COLS = 256
FF_ROWS = 512
NEG = -1e30
M_INIT = -1e29
INT_MIN = -2 ** 31
SUM_ROWS = 16

_C_Q, _C_K, _C_V, _C_IQ, _C_U, _C_VG, _C_IK, _C_IW, _C_END = (
    0, 512, 1024, 1536, 2560, 3072, 3584, 3712, 3840)


def _params(sem):
    return pltpu.CompilerParams(dimension_semantics=sem, vmem_limit_bytes=VMEM_LIMIT)


def _resident(shape):
    nd = len(shape)
    return pl.BlockSpec(shape, lambda *_: (0,) * nd, pipeline_mode=pl.Buffered(1))


def _rms(x, g):
    ms = jnp.mean(x * x, axis=-1, keepdims=True)
    return x * lax.rsqrt(ms + EPS) * g


def _dot(a, b):
    return jnp.dot(a, b, preferred_element_type=F32)


def _dot_nt(a, b):
    return lax.dot_general(a, b, (((1,), (1,)), ((), ())), preferred_element_type=F32)


def _dot_tn(a, b):
    return lax.dot_general(a, b, (((0,), (0,)), ((), ())), preferred_element_type=F32)


def _t5_bucket(rel):
    half = REL_BUCKETS // 2
    max_exact = half // 2
    ret = jnp.where(rel > 0, half, 0)
    n = jnp.abs(rel)
    nf = jnp.maximum(n, max_exact).astype(F32)
    large = max_exact + (jnp.log(nf / max_exact) / math.log(REL_MAX_DIST / max_exact)
                         * (half - max_exact)).astype(I32)
    large = jnp.minimum(large, half - 1)
    return ret + jnp.where(n < max_exact, n, large)


def _bias_kernel(rb_ref, bucket_ref, far_ref, o_ref):
    h = pl.program_id(1)

    def lookup(bucket):
        tile = jnp.zeros(bucket.shape, F32)
        for b in range(REL_BUCKETS):
            tile = jnp.where(bucket == b, rb_ref[b, h], tile)
        return tile

    o_ref[0, 0] = lookup(bucket_ref[0]) - lookup(far_ref[0])


def _bias_tiles(rel_bias):
    t = DSA_TILE
    assert t >= REL_MAX_DIST
    kk = jnp.arange(t, dtype=I32)[:, None]
    qq = jnp.arange(t, dtype=I32)[None, :]
    rel = jnp.stack([kk - qq, kk - qq - t, jnp.full((t, t), -2 * t, I32)])
    bucket = _t5_bucket(rel)
    return pl.pallas_call(
        _bias_kernel,
        grid=(2, A_HEADS),
        in_specs=[pl.BlockSpec(memory_space=pltpu.SMEM),
                  pl.BlockSpec((1, t, t), lambda d, h: (d, 0, 0)),
                  pl.BlockSpec((1, t, t), lambda d, h: (2, 0, 0))],
        out_specs=pl.BlockSpec((1, 1, t, t), lambda d, h: (d, h, 0, 0)),
        out_shape=jax.ShapeDtypeStruct((2, A_HEADS, t, t), F32),
        compiler_params=_params(("arbitrary", "arbitrary")),
        name="bias_tiles",
    )(rel_bias, bucket, bucket)


def _proj_ab_kernel(x_ref, g_ref, w_ref, gik_ref, ggm_ref, ws_ref, bs_ref,
                    q_ref, k_ref, v_ref, iq_ref, ik_ref, iw_ref, yb_ref):
    tm = x_ref.shape[0]
    h = _rms(x_ref[...], g_ref[...]).astype(BF16)

    def proj(c0, c1):
        return _dot(h, w_ref[:, c0:c1])

    q_ref[...] = (proj(_C_Q, _C_K) * (A_HEAD_DIM ** -0.5)).astype(BF16)
    k_ref[...] = proj(_C_K, _C_V).astype(BF16)
    v_ref[...] = proj(_C_V, _C_IQ).astype(BF16)
    half = (_C_U - _C_IQ) // 2
    for j in range(2):
        iq_ref[:, j * half:(j + 1) * half] = (
            proj(_C_IQ + j * half, _C_IQ + (j + 1) * half) * (IDX_DIM ** -0.5)).astype(BF16)
    ik_ref[...] = _rms(proj(_C_IK, _C_IW), gik_ref[...]).astype(BF16)
    iw_ref[...] = proj(_C_IW, _C_END)[:, :IDX_HEADS] * (IDX_HEADS ** -0.5)

    u = jax.nn.gelu(proj(_C_U, _C_VG))
    vn = _rms(jax.nn.gelu(proj(_C_VG, _C_IK)), ggm_ref[...]).astype(BF16)
    ri = lax.broadcasted_iota(I32, (B_CHUNK, B_CHUNK), 0) // CHUNK
    ci = lax.broadcasted_iota(I32, (B_CHUNK, B_CHUNK), 1) // CHUNK
    low = ri >= ci
    first = lax.broadcasted_iota(I32, (B_CHUNK, LANES), 1) < B_GROUP_DIM
    for p in range(B_GROUPS // 2):
        w_e = jnp.where(low, ws_ref[2 * p], 0.0).astype(BF16)
        w_o = jnp.where(low, ws_ref[2 * p + 1], 0.0).astype(BF16)
        b_e = bs_ref[:, 2 * p:2 * p + 1]
        b_o = bs_ref[:, 2 * p + 1:2 * p + 2]
        cols = slice(p * LANES, (p + 1) * LANES)
        for r in range(tm // B_CHUNK):
            rows = slice(r * B_CHUNK, (r + 1) * B_CHUNK)
            vp = vn[rows, cols]
            s = jnp.where(first, _dot(w_e, vp) + b_e, _dot(w_o, vp) + b_o)
            yb_ref[rows, cols] = (u[rows, cols] * s).astype(BF16)


def _proj_ab(x2, g, w, gik, ggm, ws, bs_t):
    m = x2.shape[0]
    tm = ROW_TILE
    row = lambda n: pl.BlockSpec((tm, n), lambda i: (i, 0))
    outs = [(A_WIDTH, BF16), (A_WIDTH, BF16), (A_WIDTH, BF16), (IDX_HEADS * IDX_DIM, BF16),
            (2 * IDX_DIM, BF16), (IDX_HEADS, F32), (B_WIDTH, BF16)]
    return pl.pallas_call(
        _proj_ab_kernel,
        grid=(m // tm,),
        in_specs=[row(D_MODEL), _resident(g.shape), _resident(w.shape), _resident(gik.shape),
                  _resident(ggm.shape), _resident(ws.shape), _resident(bs_t.shape)],
        out_specs=[row(n) for n, _ in outs],
        out_shape=[jax.ShapeDtypeStruct((m, n), dt) for n, dt in outs],
        compiler_params=_params(("parallel",)),
        name="proj_ab",
    )(x2, g, w, gik, ggm, ws, bs_t)


def _dsa_kernel(q_ref, k_ref, vt_ref, iq_ref, ik_ref, iwt_ref, bias_ref, o_ref,
                keys_scr, madd_scr, iqm_scr, qm_scr, ot_scr, m_scr, alpha_scr, acc_scr, s_scr,
                *, top_k):
    t = DSA_TILE
    sub = 8
    i = pl.program_id(1)
    first = lax.broadcasted_iota(I32, (t, LANES), 1) < A_HEAD_DIM
    allowed = (lax.broadcasted_iota(I32, (t, t), 0) // CHUNK
               <= lax.broadcasted_iota(I32, (t, t), 1) // CHUNK)

    def blk(kb):
        return pl.ds(pl.multiple_of(kb * t, t), t)

    for h in range(IDX_HEADS):
        pair = iq_ref[:, (h // 2) * LANES:(h // 2 + 1) * LANES]
        iqm_scr[h] = jnp.where(first if h % 2 == 0 else ~first, pair, jnp.zeros_like(pair))
    for h in range(A_HEADS):
        pair = q_ref[:, (h // 2) * LANES:(h // 2 + 1) * LANES]
        qm_scr[h] = jnp.where(first if h % 2 == 0 else ~first, pair, jnp.zeros_like(pair))

    def score_keys(kb, diag):
        ikb = ik_ref[blk(kb), :]
        acc = jnp.zeros((t, t), F32)
        for h in range(IDX_HEADS):
            s = _dot_nt(ikb, iqm_scr[h])
            acc = acc + jnp.maximum(s, 0.0) * iwt_ref[h:h + 1, :]
        bits = lax.bitcast_convert_type(acc, I32)
        key = bits ^ ((bits >> 31) & jnp.int32(0x7FFFFFFF))
        if diag:
            key = jnp.where(allowed, key, jnp.int32(INT_MIN))
        keys_scr[blk(kb), :] = key

    def score_body(kb, c):
        score_keys(kb, False)
        return c

    lax.fori_loop(0, i, score_body, 0)
    score_keys(i, True)

    @pl.when(i == 0)
    def _():
        madd_scr[0:t, :] = jnp.where(allowed, 0.0, NEG)

    @pl.when(i > 0)
    def _():
        def count(cmp, thr):
            nacc = 4

            def body(kb, cnts):
                kk = keys_scr[blk(kb), :]
                cnts = list(cnts)
                for r in range(t // sub):
                    cnts[r % nacc] = cnts[r % nacc] + jnp.where(
                        cmp(kk[r * sub:(r + 1) * sub], thr), 1.0, 0.0)
                return tuple(cnts)
            cnts = lax.fori_loop(0, i + 1, body, (jnp.zeros((sub, t), F32),) * nacc)
            return jnp.sum(sum(cnts), axis=0, keepdims=True)

        ge = lambda a, b: a >= b
        gt = lambda a, b: a > b

        def bit_body(it, thr):
            cand = thr + (jnp.int32(1) << (31 - it))
            return jnp.where(count(ge, cand) >= top_k, cand, thr)

        thr = lax.fori_loop(0, 32, bit_body, jnp.full((sub, t), INT_MIN, I32))
        n_ge = count(ge, thr)
        thr_row = thr[0:1]

        def plain_body(kb, c):
            madd_scr[blk(kb), :] = jnp.where(keys_scr[blk(kb), :] >= thr_row, 0.0, NEG)
            return c

        lax.fori_loop(0, i + 1, plain_body, 0)

        @pl.when(jnp.max(n_ge) > top_k)
        def _():
            need = top_k - count(gt, thr)
            lower = (lax.broadcasted_iota(I32, (t, t), 1)
                     < lax.broadcasted_iota(I32, (t, t), 0))
            lower = jnp.where(lower, 1.0, 0.0).astype(BF16)

            def tie_body(kb, run):
                kk = keys_scr[blk(kb), :]
                eq = jnp.where(kk == thr_row, 1.0, 0.0)
                rank = run + _dot(lower, eq.astype(BF16))
                sel = (kk > thr_row) | ((kk == thr_row) & (rank < need))
                madd_scr[blk(kb), :] = jnp.where(sel, 0.0, NEG)
                return run + jnp.sum(eq, axis=0, keepdims=True)

            lax.fori_loop(0, i + 1, tie_body, jnp.zeros((1, t), F32))

    ones = jnp.ones((SUM_ROWS, t), BF16)

    m_scr[...] = jnp.full(m_scr.shape, M_INIT, F32)
    acc_scr[...] = jnp.zeros(acc_scr.shape, F32)

    def att_block(kb, didx):
        madd = madd_scr[blk(kb), :]
        for h in range(A_HEADS):
            cols = slice((h // 2) * LANES, (h // 2 + 1) * LANES)
            s = _dot_nt(k_ref[blk(kb), cols], qm_scr[h]) + madd
            if didx is not None:
                s = s + bias_ref[didx, h]
            s_scr[h] = s
            m = m_scr[h]
            m_new = jnp.maximum(m, jnp.max(s, axis=0, keepdims=True))
            alpha_scr[h] = jnp.exp(m - m_new)
            m_scr[h] = m_new

        for h in range(A_HEADS):
            pr = jnp.exp(s_scr[h] - m_scr[h]).astype(BF16)
            v1 = jnp.concatenate(
                [vt_ref[h * A_HEAD_DIM:(h + 1) * A_HEAD_DIM, blk(kb)], ones], axis=0)
            acc_scr[h] = alpha_scr[h] * acc_scr[h] + _dot(v1, pr)

    def att_body(kb, c, didx):
        att_block(kb, didx)
        return c

    n_far = jnp.maximum(i - 1, 0)
    lax.fori_loop(0, n_far, functools.partial(att_body, didx=None), 0)
    lax.fori_loop(n_far, i, functools.partial(att_body, didx=1), 0)
    att_block(i, 0)
    for h in range(A_HEADS):
        acc = acc_scr[h]
        ot_scr[h * A_HEAD_DIM:(h + 1) * A_HEAD_DIM, :] = (
            acc[:A_HEAD_DIM] * (1.0 / acc[A_HEAD_DIM:A_HEAD_DIM + 1]))
    o_ref[...] = jnp.transpose(ot_scr[...]).astype(BF16)


def _dsa(q, k, vt, iq, ik2, iwt, bias, bsz, seq):
    t = DSA_TILE
    nq = seq // t
    top_k = min(TOPK_MAX, seq // 4)
    assert t <= top_k and t % CHUNK == 0 and seq % t == 0
    qrow = lambda n: pl.BlockSpec((t, n), lambda b, i: (b * nq + i, 0))
    full = lambda n: pl.BlockSpec((seq, n), lambda b, i: (b, 0))
    return pl.pallas_call(
        functools.partial(_dsa_kernel, top_k=top_k),
        grid=(bsz, nq),
        in_specs=[qrow(A_WIDTH), full(A_WIDTH),
                  pl.BlockSpec((A_WIDTH, seq), lambda b, i: (b, 0)),
                  qrow(IDX_HEADS * IDX_DIM), full(2 * IDX_DIM),
                  pl.BlockSpec((IDX_HEADS, t), lambda b, i: (0, b * nq + i)),
                  _resident(bias.shape)],
        out_specs=qrow(A_WIDTH),
        out_shape=jax.ShapeDtypeStruct((bsz * seq, A_WIDTH), BF16),
        scratch_shapes=[pltpu.VMEM((seq, t), I32), pltpu.VMEM((seq, t), F32),
                        pltpu.VMEM((IDX_HEADS, t, LANES), BF16),
                        pltpu.VMEM((A_HEADS, t, LANES), BF16),
                        pltpu.VMEM((A_WIDTH, t), F32),
                        pltpu.VMEM((A_HEADS, 1, t), F32),
                        pltpu.VMEM((A_HEADS, 1, t), F32),
                        pltpu.VMEM((A_HEADS, A_HEAD_DIM + SUM_ROWS, t), F32),
                        pltpu.VMEM((A_HEADS, t, t), F32)],
        compiler_params=_params(("parallel", "arbitrary")),
        name="dsa",
    )(q, k, vt, iq, ik2, iwt, bias)


HALO = 16


def _ffn_kernel(*refs, n_y, tiles_per_seq, final_norm):
    x_ref, xh_ref = refs[0], refs[1]
    y_refs = refs[2:2 + 2 * n_y]
    (wo_ref, g_ref, wup_ref, cw_ref, cb_ref, wdn_ref, gfin_ref, o_ref,
     ua_scr, ub_scr, act_scr, acc_scr) = refs[2 + 2 * n_y:]
    tm = x_ref.shape[0]

    def mixed(x_r, ys):
        acc = x_r[...]
        r0 = 0
        for y_r in ys:
            n = y_r.shape[1]
            acc = acc + _dot(y_r[...], wo_ref[r0:r0 + n, :])
            r0 += n
        return acc

    x = mixed(x_ref, y_refs[0::2])
    hn = _rms(x, g_ref[...])
    keep = jnp.where(pl.program_id(0) % tiles_per_seq == 0, 0.0, 1.0)
    hh = _rms(mixed(xh_ref, y_refs[1::2]), g_ref[...]) * keep
    he = jnp.concatenate([hh, hn], axis=0).astype(BF16)

    sm = FF_ROWS

    def conv(scr, c0):
        w = cw_ref[:, c0:c0 + FF_COLS]
        return (cb_ref[:, c0:c0 + FF_COLS]
                + w[2:3] * scr[HALO:HALO + sm]
                + w[1:2] * scr[HALO - 1:HALO - 1 + sm]
                + w[0:1] * scr[HALO - 2:HALO - 2 + sm])

    stages = [(j, s) for j in range(D_FF // FF_COLS) for s in range(tm // sm)]

    def up(n):
        j, s = stages[n]
        hs = he[s * sm:s * sm + sm + HALO]
        ua_scr[n % 2] = _dot(hs, wup_ref[:, j * FF_COLS:(j + 1) * FF_COLS])
        ub_scr[n % 2] = _dot(hs, wup_ref[:, D_FF + j * FF_COLS:D_FF + (j + 1) * FF_COLS])

    def down(n):
        j, s = stages[n]
        part = _dot(act_scr[n % 2], wdn_ref[j * FF_COLS:(j + 1) * FF_COLS, :])
        rows = slice(s * sm, (s + 1) * sm)
        if j == 0:
            acc_scr[rows] = part
        else:
            acc_scr[rows] += part

    up(0)
    for n, (j, s) in enumerate(stages):
        ca, cb = j * FF_COLS, D_FF + j * FF_COLS
        if n + 1 < len(stages):
            up(n + 1)
        if n >= 1:
            down(n - 1)
        act_scr[n % 2] = (jax.nn.silu(conv(ua_scr.at[n % 2], ca))
                          * conv(ub_scr.at[n % 2], cb)).astype(BF16)
    down(len(stages) - 1)
    y = x + acc_scr[...]
    if final_norm:
        y = _rms(y, gfin_ref[...])
    o_ref[...] = y


def _ffn(x2, ys, wo, g, wup, cw, cb, wdn, gfin, seq, final_norm):
    m = x2.shape[0]
    tm = ROW_TILE
    assert seq % tm == 0 and D_FF % FF_COLS == 0 and tm % HALO == 0
    row = lambda n: pl.BlockSpec((tm, n), lambda i: (i, 0))
    halo = lambda n: pl.BlockSpec((HALO, n), lambda i: (jnp.maximum(i * (tm // HALO) - 1, 0), 0))
    y_specs, y_args = [], []
    for y in ys:
        y_specs += [row(y.shape[1]), halo(y.shape[1])]
        y_args += [y, y]
    consts = (wo, g, wup, cw, cb, wdn, gfin)
    return pl.pallas_call(
        functools.partial(_ffn_kernel, n_y=len(ys), tiles_per_seq=seq // tm,
                          final_norm=final_norm),
        grid=(m // tm,),
        in_specs=[row(D_MODEL), halo(D_MODEL)] + y_specs + [_resident(c.shape) for c in consts],
        out_specs=row(D_MODEL),
        out_shape=jax.ShapeDtypeStruct((m, D_MODEL), F32),
        scratch_shapes=[pltpu.VMEM((2, FF_ROWS + HALO, FF_COLS), F32),
                        pltpu.VMEM((2, FF_ROWS + HALO, FF_COLS), F32),
                        pltpu.VMEM((2, FF_ROWS, FF_COLS), BF16),
                        pltpu.VMEM((tm, D_MODEL), F32)],
        compiler_params=_params(("parallel",)),
        name="conv_ffn",
    )(x2, x2, *y_args, *consts)


def _hgrn_kernel(x_ref, g_ref, w_ref, lbp_ref, gout_ref, o_ref, proj_scr, st_scr, *, layer):
    ts = x_ref.shape[0]
    kw = C_HEADS * C_KEY_DIM
    c = CHUNK
    nsub = c // HG_SUB

    @pl.when(pl.program_id(1) == 0)
    def _():
        st_scr[...] = jnp.zeros(st_scr.shape, F32)

    h = _rms(x_ref[...], g_ref[...]).astype(BF16)
    step = 512
    for n in range(proj_scr.shape[1] // step):
        proj_scr[:, n * step:(n + 1) * step] = _dot(h, w_ref[:, n * step:(n + 1) * step])

    lbp = lbp_ref[...]
    e = jnp.exp(lbp - jnp.max(lbp, axis=0, keepdims=True))
    sm = e / jnp.sum(e, axis=0, keepdims=True)
    lb = jnp.sum(sm[1:layer + 1], axis=0, keepdims=True) if layer > 0 else jnp.zeros((1, kw), F32)

    ri = lax.broadcasted_iota(I32, (c, c), 0)
    ci = lax.broadcasted_iota(I32, (c, c), 1)
    r3 = lax.broadcasted_iota(I32, (c, 3 * c), 0)
    c3 = lax.broadcasted_iota(I32, (c, 3 * c), 1) % c
    tri3 = jnp.where(r3 >= c3, 1.0, 0.0).astype(BF16)
    dmask = (ri >= ci) & (ri // HG_SUB == ci // HG_SUB)

    def front(rows):
        gg = lb + (1.0 - lb) * jax.nn.sigmoid(proj_scr[rows, kw:2 * kw])
        lg = jnp.log(gg)
        hi = lg.astype(BF16)
        r1 = lg - hi.astype(F32)
        mid = r1.astype(BF16)
        lo = (r1 - mid.astype(F32)).astype(BF16)
        b_all = _dot(tri3, jnp.concatenate([hi, mid, lo], axis=0))
        out = []
        for hd in range(C_HEADS):
            kc = slice(hd * C_KEY_DIM, (hd + 1) * C_KEY_DIM)
            q = jax.nn.silu(proj_scr[rows, hd * C_KEY_DIM:(hd + 1) * C_KEY_DIM])
            kk = 1.0 - gg[:, kc]
            b = b_all[:, kc]
            iv = proj_scr[rows, 2 * kw + hd * C_VAL_DIM:2 * kw + (hd + 1) * C_VAL_DIM].astype(BF16)
            gate = proj_scr[rows, 2 * kw + C_WIDTH + hd * C_VAL_DIM:
                            2 * kw + C_WIDTH + (hd + 1) * C_VAL_DIM]
            qs, ks = [], []
            for j in range(nsub - 1):
                e0, e1 = j * HG_SUB, (j + 1) * HG_SUB
                rj = b[e1 - 1:e1]
                qs.append(jnp.concatenate(
                    [jnp.zeros((e1, C_KEY_DIM), F32), q[e1:] * jnp.exp(b[e1:] - rj)], axis=0))
                kj = kk[e0:e1] * jnp.exp(rj - b[e0:e1])
                parts = [kj, jnp.zeros((c - e1, C_KEY_DIM), F32)]
                if e0:
                    parts = [jnp.zeros((e0, C_KEY_DIM), F32)] + parts
                ks.append(jnp.concatenate(parts, axis=0))
            a_off = _dot_nt(jnp.concatenate(qs, axis=1).astype(BF16),
                            jnp.concatenate(ks, axis=1).astype(BF16))
            bm = jnp.concatenate(
                [jnp.broadcast_to(b[j * HG_SUB + HG_SUB // 2:j * HG_SUB + HG_SUB // 2 + 1],
                                  (HG_SUB, C_KEY_DIM)) for j in range(nsub)], axis=0)
            a_dg = _dot_nt((q * jnp.exp(b - bm)).astype(BF16), (kk * jnp.exp(bm - b)).astype(BF16))
            a = (a_off + jnp.where(dmask, a_dg, 0.0)).astype(BF16)
            bl = b[c - 1:c]
            out.append([(q * jnp.exp(b)).astype(BF16), a, iv,
                        (kk * jnp.exp(bl - b)).astype(BF16), jnp.exp(bl), jax.nn.silu(gate)])
        for part in out:
            part[1] = _dot(part[1], part[2])
        return out

    def back(rows, parts):
        sts = [st_scr[hd] for hd in range(C_HEADS)]
        inters = [_dot_nt(p[0], st.astype(BF16)) for p, st in zip(parts, sts)]
        upds = [_dot_tn(p[2], p[3]) for p in parts]
        for hd, (_, intra, _, _, dec, gsil) in enumerate(parts):
            st_scr[hd] = sts[hd] * dec + upds[hd]
            o_ref[rows, hd * C_VAL_DIM:(hd + 1) * C_VAL_DIM] = (
                _rms(inters[hd] + intra, gout_ref[...]) * gsil).astype(BF16)

    def chunk_body(ic, carry):
        rows = pl.ds(pl.multiple_of(ic * c, c), c)
        back(rows, front(rows))
        return carry

    lax.fori_loop(0, ts // c, chunk_body, 0)


def _hgrn(x2, g, w, lbp, gout, bsz, seq, layer):
    ts = HG_TILE
    nt = seq // ts
    assert seq % ts == 0 and ts % CHUNK == 0 and CHUNK % HG_SUB == 0
    row = lambda n: pl.BlockSpec((ts, n), lambda b, i: (b * nt + i, 0))
    return pl.pallas_call(
        functools.partial(_hgrn_kernel, layer=layer),
        grid=(bsz, nt),
        in_specs=[row(D_MODEL), _resident(g.shape), _resident(w.shape), _resident(lbp.shape),
                  _resident(gout.shape)],
        out_specs=row(C_WIDTH),
        out_shape=jax.ShapeDtypeStruct((bsz * seq, C_WIDTH), BF16),
        scratch_shapes=[pltpu.VMEM((ts, w.shape[1]), F32),
                        pltpu.VMEM((C_HEADS, C_VAL_DIM, C_KEY_DIM), F32)],
        compiler_params=_params(("parallel", "arbitrary")),
        name="hgrn2",
    )(x2, g, w, lbp, gout)


def kernel(x, rel_bias, hgrn_lb, mix_norm, ffn_norm, final_norm, ab_w_in, ab_idx_k_norm, ab_gmlp_norm, ab_w_s, ab_b_s, ab_w_out, c_w_in, c_out_norm, c_w_out, ffn_w_up, ffn_conv_w, ffn_conv_b, ffn_w_down):
    bsz, seq, d = x.shape
    depth = mix_norm.shape[0]
    x2 = x.reshape(bsz * seq, d)
    row = lambda a: a.reshape(1, -1)
    bias = _bias_tiles(rel_bias)
    for l in range(depth):
        if l % 2 == 0:
            e = l // 2
            w = ab_w_in[e]
            offs = [0]
            for s in (A_WIDTH, A_WIDTH, A_WIDTH, IDX_HEADS * IDX_DIM, IDX_DIM, IDX_HEADS,
                      B_WIDTH, B_WIDTH):
                offs.append(offs[-1] + s)
            seg = [w[:, offs[n]:offs[n + 1]] for n in range(8)]
            w_p = jnp.concatenate(
                seg[0:4] + seg[6:8] + [seg[4], seg[4], seg[5],
                                       jnp.zeros((d, LANES - IDX_HEADS), w.dtype)],
                axis=1).astype(BF16)
            assert w_p.shape[1] == _C_END
            gik = jnp.concatenate([ab_idx_k_norm[e], ab_idx_k_norm[e]])
            q, k, v, iq, ik2, iw, y_b = _proj_ab(
                x2, row(mix_norm[l]), w_p, row(gik), row(ab_gmlp_norm[e]),
                ab_w_s[e], jnp.transpose(ab_b_s[e]))
            vt = jnp.transpose(v.reshape(bsz, seq, A_WIDTH), (0, 2, 1)).reshape(bsz * A_WIDTH, seq)
            y_a = _dsa(q, k, vt, iq, ik2, jnp.transpose(iw), bias, bsz, seq)
            ys, wo = [y_a, y_b], ab_w_out[e]
        else:
            o_i = l // 2
            og = _hgrn(x2, row(mix_norm[l]), c_w_in[o_i].astype(BF16), hgrn_lb,
                       row(c_out_norm[o_i]), bsz, seq, l)
            ys, wo = [og], c_w_out[o_i]
        x2 = _ffn(x2, ys, wo.astype(BF16), row(ffn_norm[l]), ffn_w_up[l].astype(BF16),
                  ffn_conv_w[l], row(ffn_conv_b[l]), ffn_w_down[l].astype(BF16),
                  row(final_norm), seq, l == depth - 1)
    return x2.reshape(bsz, seq, d)
```

```python
import functools
import math

import jax
import jax.numpy as jnp
from jax import lax
from jax.experimental import pallas as pl
from jax.experimental.pallas import tpu as pltpu

F32 = jnp.float32
BF16 = jnp.bfloat16
I32 = jnp.int32

D_MODEL = 1024
CHUNK = 64
A_HEADS = 8
A_HEAD_DIM = 64
A_WIDTH = A_HEADS * A_HEAD_DIM
IDX_HEADS = 16
IDX_DIM = 64
TOPK_MAX = 256
REL_BUCKETS = 32
REL_MAX_DIST = 128
B_GROUPS = 8
B_GROUP_DIM = 64
B_WIDTH = B_GROUPS * B_GROUP_DIM
B_CHUNK = 128
C_HEADS = 8
C_KEY_DIM = 128
C_VAL_DIM = 128
C_WIDTH = C_HEADS * C_VAL_DIM
D_FF = 2816
EPS = 1e-6

LANES = 128
VMEM_LIMIT = 56 * 1024 * 1024

ROW_TILE = 512
DSA_TILE = 256
HG_TILE = 256
HG_SUB = 16
FF_COLS = 256
FF_ROWS = 512
NEG = -1e30
M_INIT = -1e29
INT_MIN = -2 ** 31
SUM_ROWS = 16

_C_Q, _C_K, _C_V, _C_IQ, _C_U, _C_VG, _C_IK, _C_IW, _C_END = (
    0, 512, 1024, 1536, 2560, 3072, 3584, 3712, 3840)


def _params(sem):
    return pltpu.CompilerParams(dimension_semantics=sem, vmem_limit_bytes=VMEM_LIMIT)


def _resident(shape):
    nd = len(shape)
    return pl.BlockSpec(shape, lambda *_: (0,) * nd, pipeline_mode=pl.Buffered(1))


def _rms(x, g):
    ms = jnp.mean(x * x, axis=-1, keepdims=True)
    return x * lax.rsqrt(ms + EPS) * g


def _dot(a, b):
    return jnp.dot(a, b, preferred_element_type=F32)


def _dot_nt(a, b):
    return lax.dot_general(a, b, (((1,), (1,)), ((), ())), preferred_element_type=F32)


def _dot_tn(a, b):
    return lax.dot_general(a, b, (((0,), (0,)), ((), ())), preferred_element_type=F32)


def _t5_bucket(rel):
    half = REL_BUCKETS // 2
    max_exact = half // 2
    ret = jnp.where(rel > 0, half, 0)
    n = jnp.abs(rel)
    nf = jnp.maximum(n, max_exact).astype(F32)
    large = max_exact + (jnp.log(nf / max_exact) / math.log(REL_MAX_DIST / max_exact)
                         * (half - max_exact)).astype(I32)
    large = jnp.minimum(large, half - 1)
    return ret + jnp.where(n < max_exact, n, large)


def _bias_kernel(rb_ref, bucket_ref, far_ref, o_ref):
    h = pl.program_id(1)

    def lookup(bucket):
        tile = jnp.zeros(bucket.shape, F32)
        for b in range(REL_BUCKETS):
            tile = jnp.where(bucket == b, rb_ref[b, h], tile)
        return tile

    o_ref[0, 0] = lookup(bucket_ref[0]) - lookup(far_ref[0])


def _bias_tiles(rel_bias):
    t = DSA_TILE
    assert t >= REL_MAX_DIST
    kk = jnp.arange(t, dtype=I32)[:, None]
    qq = jnp.arange(t, dtype=I32)[None, :]
    rel = jnp.stack([kk - qq, kk - qq - t, jnp.full((t, t), -2 * t, I32)])
    bucket = _t5_bucket(rel)
    return pl.pallas_call(
        _bias_kernel,
        grid=(2, A_HEADS),
        in_specs=[pl.BlockSpec(memory_space=pltpu.SMEM),
                  pl.BlockSpec((1, t, t), lambda d, h: (d, 0, 0)),
                  pl.BlockSpec((1, t, t), lambda d, h: (2, 0, 0))],
        out_specs=pl.BlockSpec((1, 1, t, t), lambda d, h: (d, h, 0, 0)),
        out_shape=jax.ShapeDtypeStruct((2, A_HEADS, t, t), F32),
        compiler_params=_params(("arbitrary", "arbitrary")),
        name="bias_tiles",
    )(rel_bias, bucket, bucket)


def _proj_ab_kernel(x_ref, g_ref, w_ref, gik_ref, ggm_ref, ws_ref, bs_ref,
                    q_ref, k_ref, v_ref, iq_ref, ik_ref, iw_ref, yb_ref):
    tm = x_ref.shape[0]
    h = _rms(x_ref[...], g_ref[...]).astype(BF16)

    def proj(c0, c1):
        return _dot(h, w_ref[:, c0:c1])

    q_ref[...] = (proj(_C_Q, _C_K) * (A_HEAD_DIM ** -0.5)).astype(BF16)
    k_ref[...] = proj(_C_K, _C_V).astype(BF16)
    v_ref[...] = proj(_C_V, _C_IQ).astype(BF16)
    half = (_C_U - _C_IQ) // 2
    for j in range(2):
        iq_ref[:, j * half:(j + 1) * half] = (
            proj(_C_IQ + j * half, _C_IQ + (j + 1) * half) * (IDX_DIM ** -0.5)).astype(BF16)
    ik_ref[...] = _rms(proj(_C_IK, _C_IW), gik_ref[...]).astype(BF16)
    iw_ref[...] = proj(_C_IW, _C_END)[:, :IDX_HEADS] * (IDX_HEADS ** -0.5)

    u = jax.nn.gelu(proj(_C_U, _C_VG))
    vn = _rms(jax.nn.gelu(proj(_C_VG, _C_IK)), ggm_ref[...]).astype(BF16)
    ri = lax.broadcasted_iota(I32, (B_CHUNK, B_CHUNK), 0) // CHUNK
    ci = lax.broadcasted_iota(I32, (B_CHUNK, B_CHUNK), 1) // CHUNK
    low = ri >= ci
    first = lax.broadcasted_iota(I32, (B_CHUNK, LANES), 1) < B_GROUP_DIM
    for p in range(B_GROUPS // 2):
        w_e = jnp.where(low, ws_ref[2 * p], 0.0).astype(BF16)
        w_o = jnp.where(low, ws_ref[2 * p + 1], 0.0).astype(BF16)
        b_e = bs_ref[:, 2 * p:2 * p + 1]
        b_o = bs_ref[:, 2 * p + 1:2 * p + 2]
        cols = slice(p * LANES, (p + 1) * LANES)
        for r in range(tm // B_CHUNK):
            rows = slice(r * B_CHUNK, (r + 1) * B_CHUNK)
            vp = vn[rows, cols]
            s = jnp.where(first, _dot(w_e, vp) + b_e, _dot(w_o, vp) + b_o)
            yb_ref[rows, cols] = (u[rows, cols] * s).astype(BF16)


def _proj_ab(x2, g, w, gik, ggm, ws, bs_t):
    m = x2.shape[0]
    tm = ROW_TILE
    row = lambda n: pl.BlockSpec((tm, n), lambda i: (i, 0))
    outs = [(A_WIDTH, BF16), (A_WIDTH, BF16), (A_WIDTH, BF16), (IDX_HEADS * IDX_DIM, BF16),
            (2 * IDX_DIM, BF16), (IDX_HEADS, F32), (B_WIDTH, BF16)]
    return pl.pallas_call(
        _proj_ab_kernel,
        grid=(m // tm,),
        in_specs=[row(D_MODEL), _resident(g.shape), _resident(w.shape), _resident(gik.shape),
                  _resident(ggm.shape), _resident(ws.shape), _resident(bs_t.shape)],
        out_specs=[row(n) for n, _ in outs],
        out_shape=[jax.ShapeDtypeStruct((m, n), dt) for n, dt in outs],
        compiler_params=_params(("parallel",)),
        name="proj_ab",
    )(x2, g, w, gik, ggm, ws, bs_t)


def _dsa_kernel(q_ref, k_ref, vt_ref, iq_ref, ik_ref, iwt_ref, bias_ref, o_ref,
                keys_scr, madd_scr, iqm_scr, qm_scr, ot_scr, m_scr, alpha_scr, acc_scr, s_scr,
                *, top_k):
    t = DSA_TILE
    sub = 8
    i = pl.program_id(1)
    first = lax.broadcasted_iota(I32, (t, LANES), 1) < A_HEAD_DIM
    allowed = (lax.broadcasted_iota(I32, (t, t), 0) // CHUNK
               <= lax.broadcasted_iota(I32, (t, t), 1) // CHUNK)

    def blk(kb):
        return pl.ds(pl.multiple_of(kb * t, t), t)

    for h in range(IDX_HEADS):
        pair = iq_ref[:, (h // 2) * LANES:(h // 2 + 1) * LANES]
        iqm_scr[h] = jnp.where(first if h % 2 == 0 else ~first, pair, jnp.zeros_like(pair))
    for h in range(A_HEADS):
        pair = q_ref[:, (h // 2) * LANES:(h // 2 + 1) * LANES]
        qm_scr[h] = jnp.where(first if h % 2 == 0 else ~first, pair, jnp.zeros_like(pair))

    def score_keys(kb, diag):
        ikb = ik_ref[blk(kb), :]
        acc = jnp.zeros((t, t), F32)
        for h in range(IDX_HEADS):
            s = _dot_nt(ikb, iqm_scr[h])
            acc = acc + jnp.maximum(s, 0.0) * iwt_ref[h:h + 1, :]
        bits = lax.bitcast_convert_type(acc, I32)
        key = bits ^ ((bits >> 31) & jnp.int32(0x7FFFFFFF))
        if diag:
            key = jnp.where(allowed, key, jnp.int32(INT_MIN))
        keys_scr[blk(kb), :] = key

    def score_body(kb, c):
        score_keys(kb, False)
        return c

    lax.fori_loop(0, i, score_body, 0)
    score_keys(i, True)

    @pl.when(i == 0)
    def _():
        madd_scr[0:t, :] = jnp.where(allowed, 0.0, NEG)

    @pl.when(i > 0)
    def _():
        def count(cmp, thr):
            nacc = 4

            def body(kb, cnts):
                kk = keys_scr[blk(kb), :]
                cnts = list(cnts)
                for r in range(t // sub):
                    cnts[r % nacc] = cnts[r % nacc] + jnp.where(
                        cmp(kk[r * sub:(r + 1) * sub], thr), 1.0, 0.0)
                return tuple(cnts)
            cnts = lax.fori_loop(0, i + 1, body, (jnp.zeros((sub, t), F32),) * nacc)
            return jnp.sum(sum(cnts), axis=0, keepdims=True)

        ge = lambda a, b: a >= b
        gt = lambda a, b: a > b

        def bit_body(it, thr):
            cand = thr + (jnp.int32(1) << (31 - it))
            return jnp.where(count(ge, cand) >= top_k, cand, thr)

        thr = lax.fori_loop(0, 32, bit_body, jnp.full((sub, t), INT_MIN, I32))
        n_ge = count(ge, thr)
        thr_row = thr[0:1]

        def plain_body(kb, c):
            madd_scr[blk(kb), :] = jnp.where(keys_scr[blk(kb), :] >= thr_row, 0.0, NEG)
            return c

        lax.fori_loop(0, i + 1, plain_body, 0)

        @pl.when(jnp.max(n_ge) > top_k)
        def _():
            need = top_k - count(gt, thr)
            lower = (lax.broadcasted_iota(I32, (t, t), 1)
                     < lax.broadcasted_iota(I32, (t, t), 0))
            lower = jnp.where(lower, 1.0, 0.0).astype(BF16)

            def tie_body(kb, run):
                kk = keys_scr[blk(kb), :]
                eq = jnp.where(kk == thr_row, 1.0, 0.0)
                rank = run + _dot(lower, eq.astype(BF16))
                sel = (kk > thr_row) | ((kk == thr_row) & (rank < need))
                madd_scr[blk(kb), :] = jnp.where(sel, 0.0, NEG)
                return run + jnp.sum(eq, axis=0, keepdims=True)

            lax.fori_loop(0, i + 1, tie_body, jnp.zeros((1, t), F32))

    ones = jnp.ones((SUM_ROWS, t), BF16)

    m_scr[...] = jnp.full(m_scr.shape, M_INIT, F32)
    acc_scr[...] = jnp.zeros(acc_scr.shape, F32)

    def att_block(kb, didx):
        madd = madd_scr[blk(kb), :]
        for h in range(A_HEADS):
            cols = slice((h // 2) * LANES, (h // 2 + 1) * LANES)
            s = _dot_nt(k_ref[blk(kb), cols], qm_scr[h]) + madd
            if didx is not None:
                s = s + bias_ref[didx, h]
            s_scr[h] = s
            m = m_scr[h]
            m_new = jnp.maximum(m, jnp.max(s, axis=0, keepdims=True))
            alpha_scr[h] = jnp.exp(m - m_new)
            m_scr[h] = m_new

        for h in range(A_HEADS):
            pr = jnp.exp(s_scr[h] - m_scr[h]).astype(BF16)
            v1 = jnp.concatenate(
                [vt_ref[h * A_HEAD_DIM:(h + 1) * A_HEAD_DIM, blk(kb)], ones], axis=0)
            acc_scr[h] = alpha_scr[h] * acc_scr[h] + _dot(v1, pr)

    def att_body(kb, c, didx):
        att_block(kb, didx)
        return c

    n_far = jnp.maximum(i - 1, 0)
    lax.fori_loop(0, n_far, functools.partial(att_body, didx=None), 0)
    lax.fori_loop(n_far, i, functools.partial(att_body, didx=1), 0)
    att_block(i, 0)
    for h in range(A_HEADS):
        acc = acc_scr[h]
        ot_scr[h * A_HEAD_DIM:(h + 1) * A_HEAD_DIM, :] = (
            acc[:A_HEAD_DIM] * (1.0 / acc[A_HEAD_DIM:A_HEAD_DIM + 1]))
    o_ref[...] = jnp.transpose(ot_scr[...]).astype(BF16)


def _dsa(q, k, vt, iq, ik2, iwt, bias, bsz, seq):
    t = DSA_TILE
    nq = seq // t
    top_k = min(TOPK_MAX, seq // 4)
    assert t <= top_k and t % CHUNK == 0 and seq % t == 0
    qrow = lambda n: pl.BlockSpec((t, n), lambda b, i: (b * nq + i, 0))
    full = lambda n: pl.BlockSpec((seq, n), lambda b, i: (b, 0))
    return pl.pallas_call(
        functools.partial(_dsa_kernel, top_k=top_k),
        grid=(bsz, nq),
        in_specs=[qrow(A_WIDTH), full(A_WIDTH),
                  pl.BlockSpec((A_WIDTH, seq), lambda b, i: (b, 0)),
                  qrow(IDX_HEADS * IDX_DIM), full(2 * IDX_DIM),
                  pl.BlockSpec((IDX_HEADS, t), lambda b, i: (0, b * nq + i)),
                  _resident(bias.shape)],
        out_specs=qrow(A_WIDTH),
        out_shape=jax.ShapeDtypeStruct((bsz * seq, A_WIDTH), BF16),
        scratch_shapes=[pltpu.VMEM((seq, t), I32), pltpu.VMEM((seq, t), F32),
                        pltpu.VMEM((IDX_HEADS, t, LANES), BF16),
                        pltpu.VMEM((A_HEADS, t, LANES), BF16),
                        pltpu.VMEM((A_WIDTH, t), F32),
                        pltpu.VMEM((A_HEADS, 1, t), F32),
                        pltpu.VMEM((A_HEADS, 1, t), F32),
                        pltpu.VMEM((A_HEADS, A_HEAD_DIM + SUM_ROWS, t), F32),
                        pltpu.VMEM((A_HEADS, t, t), F32)],
        compiler_params=_params(("parallel", "arbitrary")),
        name="dsa",
    )(q, k, vt, iq, ik2, iwt, bias)


HALO = 16


def _ffn_kernel(*refs, n_y, tiles_per_seq, final_norm):
    x_ref, xh_ref = refs[0], refs[1]
    y_refs = refs[2:2 + 2 * n_y]
    (wo_ref, g_ref, wup_ref, cw_ref, cb_ref, wdn_ref, gfin_ref, o_ref,
     ua_scr, ub_scr, act_scr, acc_scr) = refs[2 + 2 * n_y:]
    tm = x_ref.shape[0]

    def mixed(x_r, ys):
        acc = x_r[...]
        r0 = 0
        for y_r in ys:
            n = y_r.shape[1]
            acc = acc + _dot(y_r[...], wo_ref[r0:r0 + n, :])
            r0 += n
        return acc

    x = mixed(x_ref, y_refs[0::2])
    hn = _rms(x, g_ref[...])
    keep = jnp.where(pl.program_id(0) % tiles_per_seq == 0, 0.0, 1.0)
    hh = _rms(mixed(xh_ref, y_refs[1::2]), g_ref[...]) * keep
    he = jnp.concatenate([hh, hn], axis=0).astype(BF16)

    sm = FF_ROWS

    def conv(scr, c0):
        w = cw_ref[:, c0:c0 + FF_COLS]
        return (cb_ref[:, c0:c0 + FF_COLS]
                + w[2:3] * scr[HALO:HALO + sm]
                + w[1:2] * scr[HALO - 1:HALO - 1 + sm]
                + w[0:1] * scr[HALO - 2:HALO - 2 + sm])

    stages = [(j, s) for j in range(D_FF // FF_COLS) for s in range(tm // sm)]

    def up(n):
        j, s = stages[n]
        hs = he[s * sm:s * sm + sm + HALO]
        ua_scr[n % 2] = _dot(hs, wup_ref[:, j * FF_COLS:(j + 1) * FF_COLS])
        ub_scr[n % 2] = _dot(hs, wup_ref[:, D_FF + j * FF_COLS:D_FF + (j + 1) * FF_COLS])

    def down(n):
        j, s = stages[n]
        part = _dot(act_scr[n % 2], wdn_ref[j * FF_COLS:(j + 1) * FF_COLS, :])
        rows = slice(s * sm, (s + 1) * sm)
        if j == 0:
            acc_scr[rows] = part
        else:
            acc_scr[rows] += part

    up(0)
    for n, (j, s) in enumerate(stages):
        ca, cb = j * FF_COLS, D_FF + j * FF_COLS
        if n + 1 < len(stages):
            up(n + 1)
        if n >= 1:
            down(n - 1)
        act_scr[n % 2] = (jax.nn.silu(conv(ua_scr.at[n % 2], ca))
                          * conv(ub_scr.at[n % 2], cb)).astype(BF16)
    down(len(stages) - 1)
    y = x + acc_scr[...]
    if final_norm:
        y = _rms(y, gfin_ref[...])
    o_ref[...] = y


def _ffn(x2, ys, wo, g, wup, cw, cb, wdn, gfin, seq, final_norm):
    m = x2.shape[0]
    tm = ROW_TILE
    assert seq % tm == 0 and D_FF % FF_COLS == 0 and tm % HALO == 0
    row = lambda n: pl.BlockSpec((tm, n), lambda i: (i, 0))
    halo = lambda n: pl.BlockSpec((HALO, n), lambda i: (jnp.maximum(i * (tm // HALO) - 1, 0), 0))
    y_specs, y_args = [], []
    for y in ys:
        y_specs += [row(y.shape[1]), halo(y.shape[1])]
        y_args += [y, y]
    consts = (wo, g, wup, cw, cb, wdn, gfin)
    return pl.pallas_call(
        functools.partial(_ffn_kernel, n_y=len(ys), tiles_per_seq=seq // tm,
                          final_norm=final_norm),
        grid=(m // tm,),
        in_specs=[row(D_MODEL), halo(D_MODEL)] + y_specs + [_resident(c.shape) for c in consts],
        out_specs=row(D_MODEL),
        out_shape=jax.ShapeDtypeStruct((m, D_MODEL), F32),
        scratch_shapes=[pltpu.VMEM((2, FF_ROWS + HALO, FF_COLS), F32),
                        pltpu.VMEM((2, FF_ROWS + HALO, FF_COLS), F32),
                        pltpu.VMEM((2, FF_ROWS, FF_COLS), BF16),
                        pltpu.VMEM((tm, D_MODEL), F32)],
        compiler_params=_params(("parallel",)),
        name="conv_ffn",
    )(x2, x2, *y_args, *consts)


def _hgrn_kernel(x_ref, g_ref, w_ref, lbp_ref, gout_ref, o_ref, proj_scr, st_scr, *, layer):
    ts = x_ref.shape[0]
    kw = C_HEADS * C_KEY_DIM
    c = CHUNK
    nsub = c // HG_SUB

    @pl.when(pl.program_id(1) == 0)
    def _():
        st_scr[...] = jnp.zeros(st_scr.shape, F32)

    h = _rms(x_ref[...], g_ref[...]).astype(BF16)

    def project(c0, c1):
        pw = 512
        for n in range(c0, c1, pw):
            proj_scr[:, n:n + pw] = _dot(h, w_ref[:, n:n + pw])

    lbp = lbp_ref[...]
    e = jnp.exp(lbp - jnp.max(lbp, axis=0, keepdims=True))
    sm = e / jnp.sum(e, axis=0, keepdims=True)
    lb = jnp.sum(sm[1:layer + 1], axis=0, keepdims=True) if layer > 0 else jnp.zeros((1, kw), F32)

    ri = lax.broadcasted_iota(I32, (c, c), 0)
    ci = lax.broadcasted_iota(I32, (c, c), 1)
    r3 = lax.broadcasted_iota(I32, (c, 3 * c), 0)
    c3 = lax.broadcasted_iota(I32, (c, 3 * c), 1) % c
    tri3 = jnp.where(r3 >= c3, 1.0, 0.0).astype(BF16)
    dmask = (ri >= ci) & (ri // HG_SUB == ci // HG_SUB)

    def gates(rows):
        gg = lb + (1.0 - lb) * jax.nn.sigmoid(proj_scr[rows, kw:2 * kw])
        lg = jnp.log(gg)
        hi = lg.astype(BF16)
        r1 = lg - hi.astype(F32)
        mid = r1.astype(BF16)
        lo = (r1 - mid.astype(F32)).astype(BF16)
        return gg, jnp.concatenate([hi, mid, lo], axis=0)

    def decay_products(rows, gg, b_all):
        out = []
        for hd in range(C_HEADS):
            kc = slice(hd * C_KEY_DIM, (hd + 1) * C_KEY_DIM)
            q = jax.nn.silu(proj_scr[rows, hd * C_KEY_DIM:(hd + 1) * C_KEY_DIM])
            kk = 1.0 - gg[:, kc]
            b = b_all[:, kc]
            qs, ks = [], []
            for j in range(nsub - 1):
                e0, e1 = j * HG_SUB, (j + 1) * HG_SUB
                rj = b[e1 - 1:e1]
                qs.append(jnp.concatenate(
                    [jnp.zeros((e1, C_KEY_DIM), F32), q[e1:] * jnp.exp(b[e1:] - rj)], axis=0))
                kj = kk[e0:e1] * jnp.exp(rj - b[e0:e1])
                parts = [kj, jnp.zeros((c - e1, C_KEY_DIM), F32)]
                if e0:
                    parts = [jnp.zeros((e0, C_KEY_DIM), F32)] + parts
                ks.append(jnp.concatenate(parts, axis=0))
            a_off = _dot_nt(jnp.concatenate(qs, axis=1).astype(BF16),
                            jnp.concatenate(ks, axis=1).astype(BF16))
            bm = jnp.concatenate(
                [jnp.broadcast_to(b[j * HG_SUB + HG_SUB // 2:j * HG_SUB + HG_SUB // 2 + 1],
                                  (HG_SUB, C_KEY_DIM)) for j in range(nsub)], axis=0)
            a_dg = _dot_nt((q * jnp.exp(b - bm)).astype(BF16), (kk * jnp.exp(bm - b)).astype(BF16))
            a = (a_off + jnp.where(dmask, a_dg, 0.0)).astype(BF16)
            bl = b[c - 1:c]
            out.append([(q * jnp.exp(b)).astype(BF16), a,
                        (kk * jnp.exp(bl - b)).astype(BF16), jnp.exp(bl)])
        return out

    def values(rows, hd):
        return proj_scr[rows, 2 * kw + hd * C_VAL_DIM:2 * kw + (hd + 1) * C_VAL_DIM].astype(BF16)

    def back(rows, parts, intras):
        sts = [st_scr[hd] for hd in range(C_HEADS)]
        inters = [_dot_nt(p[0], st.astype(BF16)) for p, st in zip(parts, sts)]
        upds = [_dot_tn(values(rows, hd), p[2]) for hd, p in enumerate(parts)]
        for hd, p in enumerate(parts):
            st_scr[hd] = sts[hd] * p[3] + upds[hd]
            gate = proj_scr[rows, 2 * kw + C_WIDTH + hd * C_VAL_DIM:
                            2 * kw + C_WIDTH + (hd + 1) * C_VAL_DIM]
            o_ref[rows, hd * C_VAL_DIM:(hd + 1) * C_VAL_DIM] = (
                _rms(inters[hd] + intras[hd], gout_ref[...]) * jax.nn.silu(gate)).astype(BF16)

    rows = [slice(n * c, (n + 1) * c) for n in range(ts // c)]
    project(kw, 2 * kw)
    gate_parts = [gates(r) for r in rows]
    project(0, kw)
    cums = [_dot(tri3, lg3) for _, lg3 in gate_parts]
    project(2 * kw, 2 * kw + C_WIDTH)
    parts = [decay_products(r, gg, b_all) for r, (gg, _), b_all in zip(rows, gate_parts, cums)]
    project(2 * kw + C_WIDTH, 2 * kw + 2 * C_WIDTH)
    intras = [[_dot(p[1], values(r, hd)) for hd, p in enumerate(ps)]
              for r, ps in zip(rows, parts)]
    for r, ps, it in zip(rows, parts, intras):
        back(r, ps, it)


def _hgrn(x2, g, w, lbp, gout, bsz, seq, layer):
    ts = HG_TILE
    nt = seq // ts
    assert seq % ts == 0 and ts % CHUNK == 0 and CHUNK % HG_SUB == 0
    row = lambda n: pl.BlockSpec((ts, n), lambda b, i: (b * nt + i, 0))
    return pl.pallas_call(
        functools.partial(_hgrn_kernel, layer=layer),
        grid=(bsz, nt),
        in_specs=[row(D_MODEL), _resident(g.shape), _resident(w.shape), _resident(lbp.shape),
                  _resident(gout.shape)],
        out_specs=row(C_WIDTH),
        out_shape=jax.ShapeDtypeStruct((bsz * seq, C_WIDTH), BF16),
        scratch_shapes=[pltpu.VMEM((ts, w.shape[1]), F32),
                        pltpu.VMEM((C_HEADS, C_VAL_DIM, C_KEY_DIM), F32)],
        compiler_params=_params(("parallel", "arbitrary")),
        name="hgrn2",
    )(x2, g, w, lbp, gout)


def kernel(x, rel_bias, hgrn_lb, mix_norm, ffn_norm, final_norm, ab_w_in, ab_idx_k_norm, ab_gmlp_norm, ab_w_s, ab_b_s, ab_w_out, c_w_in, c_out_norm, c_w_out, ffn_w_up, ffn_conv_w, ffn_conv_b, ffn_w_down):
    bsz, seq, d = x.shape
    depth = mix_norm.shape[0]
    x2 = x.reshape(bsz * seq, d)
    row = lambda a: a.reshape(1, -1)
    bias = _bias_tiles(rel_bias)
    for l in range(depth):
        if l % 2 == 0:
            e = l // 2
            w = ab_w_in[e]
            offs = [0]
            for s in (A_WIDTH, A_WIDTH, A_WIDTH, IDX_HEADS * IDX_DIM, IDX_DIM, IDX_HEADS,
                      B_WIDTH, B_WIDTH):
                offs.append(offs[-1] + s)
            seg = [w[:, offs[n]:offs[n + 1]] for n in range(8)]
            w_p = jnp.concatenate(
                seg[0:4] + seg[6:8] + [seg[4], seg[4], seg[5],
                                       jnp.zeros((d, LANES - IDX_HEADS), w.dtype)],
                axis=1).astype(BF16)
            assert w_p.shape[1] == _C_END
            gik = jnp.concatenate([ab_idx_k_norm[e], ab_idx_k_norm[e]])
            q, k, v, iq, ik2, iw, y_b = _proj_ab(
                x2, row(mix_norm[l]), w_p, row(gik), row(ab_gmlp_norm[e]),
                ab_w_s[e], jnp.transpose(ab_b_s[e]))
            vt = jnp.transpose(v.reshape(bsz, seq, A_WIDTH), (0, 2, 1)).reshape(bsz * A_WIDTH, seq)
            y_a = _dsa(q, k, vt, iq, ik2, jnp.transpose(iw), bias, bsz, seq)
            ys, wo = [y_a, y_b], ab_w_out[e]
        else:
            o_i = l // 2
            og = _hgrn(x2, row(mix_norm[l]), c_w_in[o_i].astype(BF16), hgrn_lb,
                       row(c_out_norm[o_i]), bsz, seq, l)
            ys, wo = [og], c_w_out[o_i]
        x2 = _ffn(x2, ys, wo.astype(BF16), row(ffn_norm[l]), ffn_w_up[l].astype(BF16),
                  ffn_conv_w[l], row(ffn_conv_b[l]), ffn_w_down[l].astype(BF16),
                  row(final_norm), seq, l == depth - 1)
    return x2.reshape(bsz, seq, d)
```

```python
import functools
import math

import jax
import jax.numpy as jnp
from jax import lax
from jax.experimental import pallas as pl
from jax.experimental.pallas import tpu as pltpu

F32 = jnp.float32
BF16 = jnp.bfloat16
I32 = jnp.int32

D_MODEL = 1024
CHUNK = 64
A_HEADS = 8
A_HEAD_DIM = 64
A_WIDTH = A_HEADS * A_HEAD_DIM
IDX_HEADS = 16
IDX_DIM = 64
TOPK_MAX = 256
REL_BUCKETS = 32
REL_MAX_DIST = 128
B_GROUPS = 8
B_GROUP_DIM = 64
B_WIDTH = B_GROUPS * B_GROUP_DIM
B_CHUNK = 128
C_HEADS = 8
C_KEY_DIM = 128
C_VAL_DIM = 128
C_WIDTH = C_HEADS * C_VAL_DIM
D_FF = 2816
EPS = 1e-6

LANES = 128
VMEM_LIMIT = 56 * 1024 * 1024

ROW_TILE = 512
DSA_TILE = 256
HG_TILE = 256
HG_SUB = 16
FF_COLS = 256
NEG = -1e30
M_INIT = -1e29
INT_MIN = -2 ** 31
SUM_ROWS = 16

_C_Q, _C_K, _C_V, _C_IQ, _C_U, _C_VG, _C_IK, _C_IW, _C_END = (
    0, 512, 1024, 1536, 2560, 3072, 3584, 3712, 3840)


def _params(sem):
    return pltpu.CompilerParams(dimension_semantics=sem, vmem_limit_bytes=VMEM_LIMIT)


def _resident(shape):
    nd = len(shape)
    return pl.BlockSpec(shape, lambda *_: (0,) * nd, pipeline_mode=pl.Buffered(1))


def _rms(x, g):
    ms = jnp.mean(x * x, axis=-1, keepdims=True)
    return x * lax.rsqrt(ms + EPS) * g


def _dot(a, b):
    return jnp.dot(a, b, preferred_element_type=F32)


def _dot_nt(a, b):
    return lax.dot_general(a, b, (((1,), (1,)), ((), ())), preferred_element_type=F32)


def _dot_tn(a, b):
    return lax.dot_general(a, b, (((0,), (0,)), ((), ())), preferred_element_type=F32)


def _t5_bucket(rel):
    half = REL_BUCKETS // 2
    max_exact = half // 2
    ret = jnp.where(rel > 0, half, 0)
    n = jnp.abs(rel)
    nf = jnp.maximum(n, max_exact).astype(F32)
    large = max_exact + (jnp.log(nf / max_exact) / math.log(REL_MAX_DIST / max_exact)
                         * (half - max_exact)).astype(I32)
    large = jnp.minimum(large, half - 1)
    return ret + jnp.where(n < max_exact, n, large)


def _bias_tiles(rel_bias):
    t = DSA_TILE
    assert t >= REL_MAX_DIST
    kk = jnp.arange(t, dtype=I32)[:, None]
    qq = jnp.arange(t, dtype=I32)[None, :]
    near = jnp.stack([kk - qq, kk - qq - t])
    tiles = jnp.transpose(rel_bias[_t5_bucket(near)], (0, 3, 1, 2))
    far = rel_bias[_t5_bucket(jnp.int32(-2 * t))]
    return tiles - far[None, :, None, None]


def _proj_ab_kernel(x_ref, g_ref, w_ref, gik_ref, ggm_ref, ws_ref, bs_ref,
                    q_ref, k_ref, v_ref, iq_ref, ik_ref, iw_ref, yb_ref):
    tm = x_ref.shape[0]
    h = _rms(x_ref[...], g_ref[...]).astype(BF16)

    def proj(c0, c1):
        return _dot(h, w_ref[:, c0:c1])

    q_ref[...] = (proj(_C_Q, _C_K) * (A_HEAD_DIM ** -0.5)).astype(BF16)
    k_ref[...] = proj(_C_K, _C_V).astype(BF16)
    v_ref[...] = proj(_C_V, _C_IQ).astype(BF16)
    half = (_C_U - _C_IQ) // 2
    for j in range(2):
        iq_ref[:, j * half:(j + 1) * half] = (
            proj(_C_IQ + j * half, _C_IQ + (j + 1) * half) * (IDX_DIM ** -0.5)).astype(BF16)
    ik_ref[...] = _rms(proj(_C_IK, _C_IW), gik_ref[...]).astype(BF16)
    iw_ref[...] = proj(_C_IW, _C_END)[:, :IDX_HEADS] * (IDX_HEADS ** -0.5)

    u = jax.nn.gelu(proj(_C_U, _C_VG))
    vn = _rms(jax.nn.gelu(proj(_C_VG, _C_IK)), ggm_ref[...]).astype(BF16)
    ri = lax.broadcasted_iota(I32, (B_CHUNK, B_CHUNK), 0) // CHUNK
    ci = lax.broadcasted_iota(I32, (B_CHUNK, B_CHUNK), 1) // CHUNK
    low = ri >= ci
    first = lax.broadcasted_iota(I32, (B_CHUNK, LANES), 1) < B_GROUP_DIM
    for p in range(B_GROUPS // 2):
        w_e = jnp.where(low, ws_ref[2 * p], 0.0).astype(BF16)
        w_o = jnp.where(low, ws_ref[2 * p + 1], 0.0).astype(BF16)
        b_e = bs_ref[:, 2 * p:2 * p + 1]
        b_o = bs_ref[:, 2 * p + 1:2 * p + 2]
        cols = slice(p * LANES, (p + 1) * LANES)
        for r in range(tm // B_CHUNK):
            rows = slice(r * B_CHUNK, (r + 1) * B_CHUNK)
            vp = vn[rows, cols]
            s = jnp.where(first, _dot(w_e, vp) + b_e, _dot(w_o, vp) + b_o)
            yb_ref[rows, cols] = (u[rows, cols] * s).astype(BF16)


def _proj_ab(x2, g, w, gik, ggm, ws, bs_t):
    m = x2.shape[0]
    tm = ROW_TILE
    row = lambda n: pl.BlockSpec((tm, n), lambda i: (i, 0))
    outs = [(A_WIDTH, BF16), (A_WIDTH, BF16), (A_WIDTH, BF16), (IDX_HEADS * IDX_DIM, BF16),
            (2 * IDX_DIM, BF16), (IDX_HEADS, F32), (B_WIDTH, BF16)]
    return pl.pallas_call(
        _proj_ab_kernel,
        grid=(m // tm,),
        in_specs=[row(D_MODEL), _resident(g.shape), _resident(w.shape), _resident(gik.shape),
                  _resident(ggm.shape), _resident(ws.shape), _resident(bs_t.shape)],
        out_specs=[row(n) for n, _ in outs],
        out_shape=[jax.ShapeDtypeStruct((m, n), dt) for n, dt in outs],
        compiler_params=_params(("parallel",)),
        name="proj_ab",
    )(x2, g, w, gik, ggm, ws, bs_t)


def _dsa_kernel(q_ref, k_ref, vt_ref, iq0_ref, ik0_ref, iwt0_ref, iqn_ref, ikn_ref, iwtn_ref,
                bias_ref, o_ref,
                keys_scr, madd_scr, iqm_scr, qm_scr, ot_scr, m_scr, alpha_scr, acc_scr, s_scr,
                *, top_k, nq, nsteps):
    t = DSA_TILE
    sub = 8
    step = pl.program_id(0)
    i = step % nq
    i1 = jnp.where(step + 1 < nsteps, (step + 1) % nq, 0)
    cur = step % 2
    nxt = 1 - cur
    first = lax.broadcasted_iota(I32, (t, LANES), 1) < A_HEAD_DIM
    allowed = (lax.broadcasted_iota(I32, (t, t), 0) // CHUNK
               <= lax.broadcasted_iota(I32, (t, t), 1) // CHUNK)

    def blk(kb):
        return pl.ds(pl.multiple_of(kb * t, t), t)

    def split_heads(src_ref, dst_scr, heads):
        for h in range(heads):
            pair = src_ref[:, (h // 2) * LANES:(h // 2 + 1) * LANES]
            dst_scr[h] = jnp.where(first if h % 2 == 0 else ~first, pair, jnp.zeros_like(pair))

    def score_keys(ik_ref, iwt_ref, kb, slot, diag):
        ikb = ik_ref[blk(kb), :]
        acc = jnp.zeros((t, t), F32)
        for h in range(IDX_HEADS):
            s = _dot_nt(ikb, iqm_scr[h])
            acc = acc + jnp.maximum(s, 0.0) * iwt_ref[h:h + 1, :]
        if diag:
            acc = jnp.where(allowed, acc, -jnp.inf)
        keys_scr[slot, blk(kb), :] = acc

    @pl.when(step == 0)
    def _():
        split_heads(iq0_ref, iqm_scr, IDX_HEADS)
        score_keys(ik0_ref, iwt0_ref, 0, 0, True)

    split_heads(iqn_ref, iqm_scr, IDX_HEADS)
    split_heads(q_ref, qm_scr, A_HEADS)

    @pl.when(i == 0)
    def _():
        madd_scr[0:t, :] = jnp.where(allowed, 0.0, NEG)

    @pl.when(i > 0)
    def _():
        def count(cmp, thr):
            nacc = 4

            def body(kb, cnts):
                kk = keys_scr[cur, blk(kb), :]
                cnts = list(cnts)
                for r in range(t // sub):
                    cnts[r % nacc] = cnts[r % nacc] + jnp.where(
                        cmp(kk[r * sub:(r + 1) * sub], thr), 1.0, 0.0)
                return tuple(cnts)
            cnts = lax.fori_loop(0, i + 1, body, (jnp.zeros((sub, t), F32),) * nacc)
            return jnp.sum(sum(cnts), axis=0, keepdims=True)

        ge = lambda a, b: a >= b
        gt = lambda a, b: a > b

        def as_score(key):
            bits = key ^ ((key >> 31) & jnp.int32(0x7FFFFFFF))
            return lax.bitcast_convert_type(bits, F32)

        def bit_body(it, key):
            cand = key + (jnp.int32(1) << (31 - it))
            return jnp.where(count(ge, as_score(cand)) >= top_k, cand, key)

        thr = as_score(lax.fori_loop(0, 32, bit_body, jnp.full((sub, t), INT_MIN, I32)))
        n_ge = count(ge, thr)
        thr_row = thr[0:1]

        def plain_body(kb, c):
            madd_scr[blk(kb), :] = jnp.where(keys_scr[cur, blk(kb), :] >= thr_row, 0.0, NEG)
            return c

        lax.fori_loop(0, i + 1, plain_body, 0)

        @pl.when(jnp.max(n_ge) > top_k)
        def _():
            need = top_k - count(gt, thr)
            lower = (lax.broadcasted_iota(I32, (t, t), 1)
                     < lax.broadcasted_iota(I32, (t, t), 0))
            lower = jnp.where(lower, 1.0, 0.0).astype(BF16)

            def tie_body(kb, run):
                kk = keys_scr[cur, blk(kb), :]
                eq = jnp.where(kk == thr_row, 1.0, 0.0)
                rank = run + _dot(lower, eq.astype(BF16))
                sel = (kk > thr_row) | ((kk == thr_row) & (rank < need))
                madd_scr[blk(kb), :] = jnp.where(sel, 0.0, NEG)
                return run + jnp.sum(eq, axis=0, keepdims=True)

            lax.fori_loop(0, i + 1, tie_body, jnp.zeros((1, t), F32))

    ones = jnp.ones((SUM_ROWS, t), BF16)

    m_scr[...] = jnp.full(m_scr.shape, M_INIT, F32)
    acc_scr[...] = jnp.zeros(acc_scr.shape, F32)

    def att_block(kb, didx):
        madd = madd_scr[blk(kb), :]
        for h in range(A_HEADS):
            cols = slice((h // 2) * LANES, (h // 2 + 1) * LANES)
            s = _dot_nt(k_ref[blk(kb), cols], qm_scr[h]) + madd
            if didx is not None:
                s = s + bias_ref[didx, h]
            s_scr[h] = s
            m = m_scr[h]
            m_new = jnp.maximum(m, jnp.max(s, axis=0, keepdims=True))
            alpha_scr[h] = jnp.exp(m - m_new)
            m_scr[h] = m_new

        score_keys(ikn_ref, iwtn_ref, kb, nxt, False)

        for h in range(A_HEADS):
            pr = jnp.exp(s_scr[h] - m_scr[h]).astype(BF16)
            v1 = jnp.concatenate(
                [vt_ref[h * A_HEAD_DIM:(h + 1) * A_HEAD_DIM, blk(kb)], ones], axis=0)
            acc_scr[h] = alpha_scr[h] * acc_scr[h] + _dot(v1, pr)

    def att_body(kb, c, didx):
        att_block(kb, didx)
        return c

    n_far = jnp.maximum(i - 1, 0)
    lax.fori_loop(0, n_far, functools.partial(att_body, didx=None), 0)
    lax.fori_loop(n_far, i, functools.partial(att_body, didx=1), 0)
    att_block(i, 0)
    score_keys(ikn_ref, iwtn_ref, i1, nxt, True)
    for h in range(A_HEADS):
        acc = acc_scr[h]
        ot_scr[h * A_HEAD_DIM:(h + 1) * A_HEAD_DIM, :] = (
            acc[:A_HEAD_DIM] * (1.0 / acc[A_HEAD_DIM:A_HEAD_DIM + 1]))
    o_ref[...] = jnp.transpose(ot_scr[...]).astype(BF16)


def _dsa(q, k, vt, iq, ik2, iwt, bias, bsz, seq):
    t = DSA_TILE
    nq = seq // t
    top_k = min(TOPK_MAX, seq // 4)
    assert t <= top_k and t % CHUNK == 0 and seq % t == 0
    nsteps = bsz * nq
    nxt = lambda s: jnp.minimum(s + 1, nsteps - 1)
    iq_w, ik_w = IDX_HEADS * IDX_DIM, 2 * IDX_DIM
    return pl.pallas_call(
        functools.partial(_dsa_kernel, top_k=top_k, nq=nq, nsteps=nsteps),
        grid=(nsteps,),
        in_specs=[pl.BlockSpec((t, A_WIDTH), lambda s: (s, 0)),
                  pl.BlockSpec((seq, A_WIDTH), lambda s: (s // nq, 0)),
                  pl.BlockSpec((A_WIDTH, seq), lambda s: (s // nq, 0)),
                  pl.BlockSpec((t, iq_w), lambda s: (0, 0)),
                  pl.BlockSpec((seq, ik_w), lambda s: (0, 0)),
                  pl.BlockSpec((IDX_HEADS, t), lambda s: (0, 0)),
                  pl.BlockSpec((t, iq_w), lambda s: (nxt(s), 0)),
                  pl.BlockSpec((seq, ik_w), lambda s: (nxt(s) // nq, 0)),
                  pl.BlockSpec((IDX_HEADS, t), lambda s: (0, nxt(s))),
                  _resident(bias.shape)],
        out_specs=pl.BlockSpec((t, A_WIDTH), lambda s: (s, 0)),
        out_shape=jax.ShapeDtypeStruct((bsz * seq, A_WIDTH), BF16),
        scratch_shapes=[pltpu.VMEM((2, seq, t), F32), pltpu.VMEM((seq, t), F32),
                        pltpu.VMEM((IDX_HEADS, t, LANES), BF16),
                        pltpu.VMEM((A_HEADS, t, LANES), BF16),
                        pltpu.VMEM((A_WIDTH, t), F32),
                        pltpu.VMEM((A_HEADS, 1, t), F32),
                        pltpu.VMEM((A_HEADS, 1, t), F32),
                        pltpu.VMEM((A_HEADS, A_HEAD_DIM + SUM_ROWS, t), F32),
                        pltpu.VMEM((A_HEADS, t, t), F32)],
        compiler_params=_params(("arbitrary",)),
        name="dsa",
    )(q, k, vt, iq, ik2, iwt, iq, ik2, iwt, bias)


HALO = 16


def _ffn_kernel(*refs, n_y, tiles_per_seq, final_norm):
    x_ref, xh_ref = refs[0], refs[1]
    y_refs = refs[2:2 + 2 * n_y]
    (wo_ref, g_ref, wup_ref, cw_ref, cb_ref, wdn_ref, gfin_ref, o_ref,
     ua_scr, ub_scr, act_scr, acc_scr) = refs[2 + 2 * n_y:]
    tm = x_ref.shape[0]

    half = tm // 2
    halves = (slice(0, half), slice(half, tm))

    def mixed(x_r, ys, rows):
        acc = x_r[rows]
        r0 = 0
        for y_r in ys:
            n = y_r.shape[1]
            acc = acc + _dot(y_r[rows], wo_ref[r0:r0 + n, :])
            r0 += n
        return acc

    xh = mixed(xh_ref, y_refs[1::2], slice(0, HALO))
    xs = [mixed(x_ref, y_refs[0::2], r) for r in halves]
    keep = jnp.where(pl.program_id(0) % tiles_per_seq == 0, 0.0, 1.0)
    hh = (_rms(xh, g_ref[...]) * keep).astype(BF16)

    def conv(scr, c0, rows):
        w = cw_ref[:, c0:c0 + FF_COLS]
        r0, r1 = HALO + rows.start, HALO + rows.stop
        return (cb_ref[:, c0:c0 + FF_COLS]
                + w[2:3] * scr[r0:r1]
                + w[1:2] * scr[r0 - 1:r1 - 1]
                + w[0:1] * scr[r0 - 2:r1 - 2])

    def gate(j, rows):
        ca, cb = j * FF_COLS, D_FF + j * FF_COLS
        act_scr[j % 2, rows] = (jax.nn.silu(conv(ua_scr.at[j % 2], ca, rows))
                                * conv(ub_scr.at[j % 2], cb, rows)).astype(BF16)

    nj = D_FF // FF_COLS

    def up(j, hs, rows=slice(None)):
        ua_scr[j % 2, rows] = _dot(hs, wup_ref[:, j * FF_COLS:(j + 1) * FF_COLS])
        ub_scr[j % 2, rows] = _dot(hs, wup_ref[:, D_FF + j * FF_COLS:D_FF + (j + 1) * FF_COLS])

    def down(j, rows=slice(None)):
        return _dot(act_scr[j % 2, rows], wdn_ref[j * FF_COLS:(j + 1) * FF_COLS, :])

    hn0 = _rms(xs[0], g_ref[...]).astype(BF16)
    up(0, jnp.concatenate([hh, hn0], axis=0), slice(0, half + HALO))
    hn1 = _rms(xs[1], g_ref[...]).astype(BF16)
    up(0, jnp.concatenate([hn0[half - HALO:], hn1], axis=0), slice(half, tm + HALO))
    he = jnp.concatenate([hh, hn0, hn1], axis=0)

    for j in range(nj):
        if j + 1 < nj:
            up(j + 1, he)
        if j == 1:
            acc_scr[...] = down(0)
        elif j > 1:
            acc_scr[...] += down(j - 1)
        if j + 1 < nj:
            gate(j, slice(0, tm))
    for r, x in zip(halves, xs):
        gate(nj - 1, r)
        acc_scr[r] += down(nj - 1, r)
        y = x + acc_scr[r]
        if final_norm:
            y = _rms(y, gfin_ref[...])
        o_ref[r] = y


def _ffn(x2, ys, wo, g, wup, cw, cb, wdn, gfin, seq, final_norm):
    m = x2.shape[0]
    tm = ROW_TILE
    assert seq % tm == 0 and D_FF % FF_COLS == 0 and tm % HALO == 0
    row = lambda n: pl.BlockSpec((tm, n), lambda i: (i, 0))
    halo = lambda n: pl.BlockSpec((HALO, n), lambda i: (jnp.maximum(i * (tm // HALO) - 1, 0), 0))
    y_specs, y_args = [], []
    for y in ys:
        y_specs += [row(y.shape[1]), halo(y.shape[1])]
        y_args += [y, y]
    consts = (wo, g, wup, cw, cb, wdn, gfin)
    return pl.pallas_call(
        functools.partial(_ffn_kernel, n_y=len(ys), tiles_per_seq=seq // tm,
                          final_norm=final_norm),
        grid=(m // tm,),
        in_specs=[row(D_MODEL), halo(D_MODEL)] + y_specs + [_resident(c.shape) for c in consts],
        out_specs=row(D_MODEL),
        out_shape=jax.ShapeDtypeStruct((m, D_MODEL), F32),
        scratch_shapes=[pltpu.VMEM((2, tm + HALO, FF_COLS), F32),
                        pltpu.VMEM((2, tm + HALO, FF_COLS), F32),
                        pltpu.VMEM((2, tm, FF_COLS), BF16),
                        pltpu.VMEM((tm, D_MODEL), F32)],
        compiler_params=_params(("parallel",)),
        name="conv_ffn",
    )(x2, x2, *y_args, *consts)


def _hgrn_kernel(x_ref, g_ref, w_ref, lbp_ref, gout_ref, o_ref, proj_scr, st_scr, *, layer):
    ts = x_ref.shape[0]
    kw = C_HEADS * C_KEY_DIM
    c = CHUNK
    nsub = c // HG_SUB

    @pl.when(pl.program_id(1) == 0)
    def _():
        st_scr[...] = jnp.zeros(st_scr.shape, F32)

    h = _rms(x_ref[...], g_ref[...]).astype(BF16)

    def project(c0, c1):
        pw = 512
        for n in range(c0, c1, pw):
            proj_scr[:, n:n + pw] = _dot(h, w_ref[:, n:n + pw])

    lbp = lbp_ref[...]
    e = jnp.exp(lbp - jnp.max(lbp, axis=0, keepdims=True))
    sm = e / jnp.sum(e, axis=0, keepdims=True)
    lb = jnp.sum(sm[1:layer + 1], axis=0, keepdims=True) if layer > 0 else jnp.zeros((1, kw), F32)

    ri = lax.broadcasted_iota(I32, (c, c), 0)
    ci = lax.broadcasted_iota(I32, (c, c), 1)
    r3 = lax.broadcasted_iota(I32, (c, 3 * c), 0)
    c3 = lax.broadcasted_iota(I32, (c, 3 * c), 1) % c
    tri3 = jnp.where(r3 >= c3, 1.0, 0.0).astype(BF16)
    dmask = (ri >= ci) & (ri // HG_SUB == ci // HG_SUB)

    def gates(rows):
        gg = lb + (1.0 - lb) * jax.nn.sigmoid(proj_scr[rows, kw:2 * kw])
        lg = jnp.log(gg)
        hi = lg.astype(BF16)
        r1 = lg - hi.astype(F32)
        mid = r1.astype(BF16)
        lo = (r1 - mid.astype(F32)).astype(BF16)
        return gg, jnp.concatenate([hi, mid, lo], axis=0)

    def decay_products(rows, gg, b_all):
        out = []
        for hd in range(C_HEADS):
            kc = slice(hd * C_KEY_DIM, (hd + 1) * C_KEY_DIM)
            q = jax.nn.silu(proj_scr[rows, hd * C_KEY_DIM:(hd + 1) * C_KEY_DIM])
            kk = 1.0 - gg[:, kc]
            b = b_all[:, kc]
            qs, ks = [], []
            for j in range(nsub - 1):
                e0, e1 = j * HG_SUB, (j + 1) * HG_SUB
                rj = b[e1 - 1:e1]
                qs.append(jnp.concatenate(
                    [jnp.zeros((e1, C_KEY_DIM), F32), q[e1:] * jnp.exp(b[e1:] - rj)], axis=0))
                kj = kk[e0:e1] * jnp.exp(rj - b[e0:e1])
                parts = [kj, jnp.zeros((c - e1, C_KEY_DIM), F32)]
                if e0:
                    parts = [jnp.zeros((e0, C_KEY_DIM), F32)] + parts
                ks.append(jnp.concatenate(parts, axis=0))
            a_off = _dot_nt(jnp.concatenate(qs, axis=1).astype(BF16),
                            jnp.concatenate(ks, axis=1).astype(BF16))
            bm = jnp.concatenate(
                [jnp.broadcast_to(b[j * HG_SUB + HG_SUB // 2:j * HG_SUB + HG_SUB // 2 + 1],
                                  (HG_SUB, C_KEY_DIM)) for j in range(nsub)], axis=0)
            a_dg = _dot_nt((q * jnp.exp(b - bm)).astype(BF16), (kk * jnp.exp(bm - b)).astype(BF16))
            a = (a_off + jnp.where(dmask, a_dg, 0.0)).astype(BF16)
            bl = b[c - 1:c]
            out.append([(q * jnp.exp(b)).astype(BF16), a,
                        (kk * jnp.exp(bl - b)).astype(BF16), jnp.exp(bl)])
        return out

    def values(rows, hd):
        return proj_scr[rows, 2 * kw + hd * C_VAL_DIM:2 * kw + (hd + 1) * C_VAL_DIM].astype(BF16)

    def back(rows, parts, intras):
        sts = [st_scr[hd] for hd in range(C_HEADS)]
        inters = [_dot_nt(p[0], st.astype(BF16)) for p, st in zip(parts, sts)]
        upds = [_dot_tn(values(rows, hd), p[2]) for hd, p in enumerate(parts)]
        for hd, p in enumerate(parts):
            st_scr[hd] = sts[hd] * p[3] + upds[hd]
            gate = proj_scr[rows, 2 * kw + C_WIDTH + hd * C_VAL_DIM:
                            2 * kw + C_WIDTH + (hd + 1) * C_VAL_DIM]
            o_ref[rows, hd * C_VAL_DIM:(hd + 1) * C_VAL_DIM] = (
                _rms(inters[hd] + intras[hd], gout_ref[...]) * jax.nn.silu(gate)).astype(BF16)

    rows = [slice(n * c, (n + 1) * c) for n in range(ts // c)]
    project(kw, 2 * kw)
    gate_parts = [gates(r) for r in rows]
    project(0, kw)
    cums = [_dot(tri3, lg3) for _, lg3 in gate_parts]
    project(2 * kw, 2 * kw + C_WIDTH)
    parts = [decay_products(r, gg, b_all) for r, (gg, _), b_all in zip(rows, gate_parts, cums)]
    project(2 * kw + C_WIDTH, 2 * kw + 2 * C_WIDTH)
    intras = [[_dot(p[1], values(r, hd)) for hd, p in enumerate(ps)]
              for r, ps in zip(rows, parts)]
    for r, ps, it in zip(rows, parts, intras):
        back(r, ps, it)


def _hgrn(x2, g, w, lbp, gout, bsz, seq, layer):
    ts = HG_TILE
    nt = seq // ts
    assert seq % ts == 0 and ts % CHUNK == 0 and CHUNK % HG_SUB == 0
    row = lambda n: pl.BlockSpec((ts, n), lambda b, i: (b * nt + i, 0))
    return pl.pallas_call(
        functools.partial(_hgrn_kernel, layer=layer),
        grid=(bsz, nt),
        in_specs=[row(D_MODEL), _resident(g.shape), _resident(w.shape), _resident(lbp.shape),
                  _resident(gout.shape)],
        out_specs=row(C_WIDTH),
        out_shape=jax.ShapeDtypeStruct((bsz * seq, C_WIDTH), BF16),
        scratch_shapes=[pltpu.VMEM((ts, w.shape[1]), F32),
                        pltpu.VMEM((C_HEADS, C_VAL_DIM, C_KEY_DIM), F32)],
        compiler_params=_params(("parallel", "arbitrary")),
        name="hgrn2",
    )(x2, g, w, lbp, gout)


def kernel(x, rel_bias, hgrn_lb, mix_norm, ffn_norm, final_norm, ab_w_in, ab_idx_k_norm, ab_gmlp_norm, ab_w_s, ab_b_s, ab_w_out, c_w_in, c_out_norm, c_w_out, ffn_w_up, ffn_conv_w, ffn_conv_b, ffn_w_down):
    bsz, seq, d = x.shape
    depth = mix_norm.shape[0]
    x2 = x.reshape(bsz * seq, d)
    row = lambda a: a.reshape(1, -1)
    bias = _bias_tiles(rel_bias)
    for l in range(depth):
        if l % 2 == 0:
            e = l // 2
            w = ab_w_in[e]
            offs = [0]
            for s in (A_WIDTH, A_WIDTH, A_WIDTH, IDX_HEADS * IDX_DIM, IDX_DIM, IDX_HEADS,
                      B_WIDTH, B_WIDTH):
                offs.append(offs[-1] + s)
            seg = [w[:, offs[n]:offs[n + 1]] for n in range(8)]
            w_p = jnp.concatenate(
                seg[0:4] + seg[6:8] + [seg[4], seg[4], seg[5],
                                       jnp.zeros((d, LANES - IDX_HEADS), w.dtype)],
                axis=1).astype(BF16)
            assert w_p.shape[1] == _C_END
            gik = jnp.concatenate([ab_idx_k_norm[e], ab_idx_k_norm[e]])
            q, k, v, iq, ik2, iw, y_b = _proj_ab(
                x2, row(mix_norm[l]), w_p, row(gik), row(ab_gmlp_norm[e]),
                ab_w_s[e], jnp.transpose(ab_b_s[e]))
            vt = jnp.transpose(v.reshape(bsz, seq, A_WIDTH), (0, 2, 1)).reshape(bsz * A_WIDTH, seq)
            y_a = _dsa(q, k, vt, iq, ik2, jnp.transpose(iw), bias, bsz, seq)
            ys, wo = [y_a, y_b], ab_w_out[e]
        else:
            o_i = l // 2
            og = _hgrn(x2, row(mix_norm[l]), c_w_in[o_i].astype(BF16), hgrn_lb,
                       row(c_out_norm[o_i]), bsz, seq, l)
            ys, wo = [og], c_w_out[o_i]
        x2 = _ffn(x2, ys, wo.astype(BF16), row(ffn_norm[l]), ffn_w_up[l].astype(BF16),
                  ffn_conv_w[l], row(ffn_conv_b[l]), ffn_w_down[l].astype(BF16),
                  row(final_norm), seq, l == depth - 1)
    return x2.reshape(bsz, seq, d)
```

```python
import functools
import math

import jax
import jax.numpy as jnp
from jax import lax
from jax.experimental import pallas as pl
from jax.experimental.pallas import tpu as pltpu

F32 = jnp.float32
BF16 = jnp.bfloat16
I32 = jnp.int32

D_MODEL = 1024
CHUNK = 64
A_HEADS = 8
A_HEAD_DIM = 64
A_WIDTH = A_HEADS * A_HEAD_DIM
IDX_HEADS = 16
IDX_DIM = 64
TOPK_MAX = 256
REL_BUCKETS = 32
REL_MAX_DIST = 128
B_GROUPS = 8
B_GROUP_DIM = 64
B_WIDTH = B_GROUPS * B_GROUP_DIM
B_CHUNK = 128
C_HEADS = 8
C_KEY_DIM = 128
C_VAL_DIM = 128
C_WIDTH = C_HEADS * C_VAL_DIM
D_FF = 2816
EPS = 1e-6

LANES = 128
VMEM_LIMIT = 56 * 1024 * 1024

ROW_TILE = 512
DSA_TILE = 256
HG_TILE = 256
HG_SUB = 16
FF_COLS = 256
NEG = -1e30
M_INIT = -1e29
INT_MIN = -2 ** 31
SUM_ROWS = 16

_C_Q, _C_K, _C_V, _C_IQ, _C_U, _C_VG, _C_IK, _C_IW, _C_END = (
    0, 512, 1024, 1536, 2560, 3072, 3584, 3712, 3840)


def _params(sem):
    return pltpu.CompilerParams(dimension_semantics=sem, vmem_limit_bytes=VMEM_LIMIT)


def _resident(shape):
    nd = len(shape)
    return pl.BlockSpec(shape, lambda *_: (0,) * nd, pipeline_mode=pl.Buffered(1))


def _rms(x, g):
    ms = jnp.mean(x * x, axis=-1, keepdims=True)
    return x * lax.rsqrt(ms + EPS) * g


def _dot(a, b):
    return jnp.dot(a, b, preferred_element_type=F32)


def _dot_nt(a, b):
    return lax.dot_general(a, b, (((1,), (1,)), ((), ())), preferred_element_type=F32)


def _dot_tn(a, b):
    return lax.dot_general(a, b, (((0,), (0,)), ((), ())), preferred_element_type=F32)


def _t5_bucket(rel):
    half = REL_BUCKETS // 2
    max_exact = half // 2
    ret = jnp.where(rel > 0, half, 0)
    n = jnp.abs(rel)
    nf = jnp.maximum(n, max_exact).astype(F32)
    large = max_exact + (jnp.log(nf / max_exact) / math.log(REL_MAX_DIST / max_exact)
                         * (half - max_exact)).astype(I32)
    large = jnp.minimum(large, half - 1)
    return ret + jnp.where(n < max_exact, n, large)


def _bias_kernel(row_ref, o_ref):
    t = DSA_TILE
    x = jnp.broadcast_to(row_ref[0, 0], (t, 2 * t))
    o_ref[0, 0] = pltpu.roll(x, 0, 1, stride=1, stride_axis=0)[:, :t]


def _bias_tiles(rel_bias):
    t = DSA_TILE
    assert t >= REL_MAX_DIST
    m = jnp.arange(2 * t, dtype=I32)
    qk = jnp.where(m < t, m, m - 2 * t)
    rel = jnp.stack([-qk, -qk - t])
    table = rel_bias[_t5_bucket(rel)] - rel_bias[_t5_bucket(jnp.int32(-2 * t))]
    rows = jnp.transpose(table, (0, 2, 1)).reshape(2, A_HEADS, 1, 2 * t)
    return pl.pallas_call(
        _bias_kernel,
        grid=(2, A_HEADS),
        in_specs=[pl.BlockSpec((1, 1, 1, 2 * t), lambda d, h: (d, h, 0, 0))],
        out_specs=pl.BlockSpec((1, 1, t, t), lambda d, h: (d, h, 0, 0)),
        out_shape=jax.ShapeDtypeStruct((2, A_HEADS, t, t), F32),
        compiler_params=_params(("arbitrary", "arbitrary")),
        name="bias_tiles",
    )(rows)


def _proj_ab_kernel(x_ref, g_ref, w_ref, gik_ref, ggm_ref, ws_ref, bs_ref,
                    q_ref, k_ref, v_ref, iq_ref, ik_ref, iw_ref, yb_ref):
    tm = x_ref.shape[0]
    h = _rms(x_ref[...], g_ref[...]).astype(BF16)

    def proj(c0, c1):
        return _dot(h, w_ref[:, c0:c1])

    q_ref[...] = (proj(_C_Q, _C_K) * (A_HEAD_DIM ** -0.5)).astype(BF16)
    k_ref[...] = proj(_C_K, _C_V).astype(BF16)
    v_ref[...] = proj(_C_V, _C_IQ).astype(BF16)
    half = (_C_U - _C_IQ) // 2
    for j in range(2):
        iq_ref[:, j * half:(j + 1) * half] = (
            proj(_C_IQ + j * half, _C_IQ + (j + 1) * half) * (IDX_DIM ** -0.5)).astype(BF16)
    ik_ref[...] = _rms(proj(_C_IK, _C_IW), gik_ref[...]).astype(BF16)
    iw_ref[...] = proj(_C_IW, _C_END)[:, :IDX_HEADS] * (IDX_HEADS ** -0.5)

    u = jax.nn.gelu(proj(_C_U, _C_VG))
    vn = _rms(jax.nn.gelu(proj(_C_VG, _C_IK)), ggm_ref[...]).astype(BF16)
    ri = lax.broadcasted_iota(I32, (B_CHUNK, B_CHUNK), 0) // CHUNK
    ci = lax.broadcasted_iota(I32, (B_CHUNK, B_CHUNK), 1) // CHUNK
    low = ri >= ci
    first = lax.broadcasted_iota(I32, (B_CHUNK, LANES), 1) < B_GROUP_DIM
    for p in range(B_GROUPS // 2):
        w_e = jnp.where(low, ws_ref[2 * p], 0.0).astype(BF16)
        w_o = jnp.where(low, ws_ref[2 * p + 1], 0.0).astype(BF16)
        b_e = bs_ref[:, 2 * p:2 * p + 1]
        b_o = bs_ref[:, 2 * p + 1:2 * p + 2]
        cols = slice(p * LANES, (p + 1) * LANES)
        for r in range(tm // B_CHUNK):
            rows = slice(r * B_CHUNK, (r + 1) * B_CHUNK)
            vp = vn[rows, cols]
            s = jnp.where(first, _dot(w_e, vp) + b_e, _dot(w_o, vp) + b_o)
            yb_ref[rows, cols] = (u[rows, cols] * s).astype(BF16)


def _proj_ab(x2, g, w, gik, ggm, ws, bs_t):
    m = x2.shape[0]
    tm = ROW_TILE
    row = lambda n: pl.BlockSpec((tm, n), lambda i: (i, 0))
    outs = [(A_WIDTH, BF16), (A_WIDTH, BF16), (A_WIDTH, BF16), (IDX_HEADS * IDX_DIM, BF16),
            (2 * IDX_DIM, BF16), (IDX_HEADS, F32), (B_WIDTH, BF16)]
    return pl.pallas_call(
        _proj_ab_kernel,
        grid=(m // tm,),
        in_specs=[row(D_MODEL), _resident(g.shape), _resident(w.shape), _resident(gik.shape),
                  _resident(ggm.shape), _resident(ws.shape), _resident(bs_t.shape)],
        out_specs=[row(n) for n, _ in outs],
        out_shape=[jax.ShapeDtypeStruct((m, n), dt) for n, dt in outs],
        compiler_params=_params(("parallel",)),
        name="proj_ab",
    )(x2, g, w, gik, ggm, ws, bs_t)


def _dsa_kernel(q_ref, k_ref, vt_ref, iq0_ref, ik0_ref, iwt0_ref, iqn_ref, ikn_ref, iwtn_ref,
                bias_ref, o_ref,
                keys_scr, madd_scr, iqm_scr, qm_scr, ot_scr, m_scr, alpha_scr, acc_scr, s_scr,
                *, top_k, nq, nsteps):
    t = DSA_TILE
    sub = 8
    step = pl.program_id(0)
    i = step % nq
    i1 = jnp.where(step + 1 < nsteps, (step + 1) % nq, 0)
    cur = step % 2
    nxt = 1 - cur
    first = lax.broadcasted_iota(I32, (t, LANES), 1) < A_HEAD_DIM
    allowed = (lax.broadcasted_iota(I32, (t, t), 0) // CHUNK
               <= lax.broadcasted_iota(I32, (t, t), 1) // CHUNK)

    def blk(kb):
        return pl.ds(pl.multiple_of(kb * t, t), t)

    def split_heads(src_ref, dst_scr, heads):
        for h in range(heads):
            pair = src_ref[:, (h // 2) * LANES:(h // 2 + 1) * LANES]
            dst_scr[h] = jnp.where(first if h % 2 == 0 else ~first, pair, jnp.zeros_like(pair))

    def score_keys(ik_ref, iwt_ref, kb, slot, diag):
        ikb = ik_ref[blk(kb), :]
        acc = jnp.zeros((t, t), F32)
        for h in range(IDX_HEADS):
            s = _dot_nt(ikb, iqm_scr[h])
            acc = acc + jnp.maximum(s, 0.0) * iwt_ref[h:h + 1, :]
        if diag:
            acc = jnp.where(allowed, acc, -jnp.inf)
        keys_scr[slot, blk(kb), :] = acc

    @pl.when(step == 0)
    def _():
        split_heads(iq0_ref, iqm_scr, IDX_HEADS)
        score_keys(ik0_ref, iwt0_ref, 0, 0, True)

    split_heads(iqn_ref, iqm_scr, IDX_HEADS)
    split_heads(q_ref, qm_scr, A_HEADS)

    @pl.when(i == 0)
    def _():
        madd_scr[0:t, :] = jnp.where(allowed, 0.0, NEG)

    @pl.when(i > 0)
    def _():
        def count(cmp, thr):
            nacc = 4

            def body(kb, cnts):
                kk = keys_scr[cur, blk(kb), :]
                cnts = list(cnts)
                for r in range(t // sub):
                    cnts[r % nacc] = cnts[r % nacc] + jnp.where(
                        cmp(kk[r * sub:(r + 1) * sub], thr), 1.0, 0.0)
                return tuple(cnts)
            cnts = lax.fori_loop(0, i + 1, body, (jnp.zeros((sub, t), F32),) * nacc)
            return jnp.sum(sum(cnts), axis=0, keepdims=True)

        ge = lambda a, b: a >= b
        gt = lambda a, b: a > b

        def as_score(key):
            bits = key ^ ((key >> 31) & jnp.int32(0x7FFFFFFF))
            return lax.bitcast_convert_type(bits, F32)

        def bit_body(it, key):
            cand = key + (jnp.int32(1) << (31 - it))
            return jnp.where(count(ge, as_score(cand)) >= top_k, cand, key)

        thr = as_score(lax.fori_loop(0, 32, bit_body, jnp.full((sub, t), INT_MIN, I32)))
        n_ge = count(ge, thr)
        thr_row = thr[0:1]

        def plain_body(kb, c):
            madd_scr[blk(kb), :] = jnp.where(keys_scr[cur, blk(kb), :] >= thr_row, 0.0, NEG)
            return c

        lax.fori_loop(0, i + 1, plain_body, 0)

        @pl.when(jnp.max(n_ge) > top_k)
        def _():
            need = top_k - count(gt, thr)
            lower = (lax.broadcasted_iota(I32, (t, t), 1)
                     < lax.broadcasted_iota(I32, (t, t), 0))
            lower = jnp.where(lower, 1.0, 0.0).astype(BF16)

            def tie_body(kb, run):
                kk = keys_scr[cur, blk(kb), :]
                eq = jnp.where(kk == thr_row, 1.0, 0.0)
                rank = run + _dot(lower, eq.astype(BF16))
                sel = (kk > thr_row) | ((kk == thr_row) & (rank < need))
                madd_scr[blk(kb), :] = jnp.where(sel, 0.0, NEG)
                return run + jnp.sum(eq, axis=0, keepdims=True)

            lax.fori_loop(0, i + 1, tie_body, jnp.zeros((1, t), F32))

    ones = jnp.ones((SUM_ROWS, t), BF16)

    m_scr[...] = jnp.full(m_scr.shape, M_INIT, F32)
    acc_scr[...] = jnp.zeros(acc_scr.shape, F32)

    def att_block(kb, didx):
        madd = madd_scr[blk(kb), :]
        for h in range(A_HEADS):
            cols = slice((h // 2) * LANES, (h // 2 + 1) * LANES)
            s = _dot_nt(k_ref[blk(kb), cols], qm_scr[h]) + madd
            if didx is not None:
                s = s + bias_ref[didx, h]
            s_scr[h] = s
            m = m_scr[h]
            m_new = jnp.maximum(m, jnp.max(s, axis=0, keepdims=True))
            alpha_scr[h] = jnp.exp(m - m_new)
            m_scr[h] = m_new

        score_keys(ikn_ref, iwtn_ref, kb, nxt, False)

        for h in range(A_HEADS):
            pr = jnp.exp(s_scr[h] - m_scr[h]).astype(BF16)
            v1 = jnp.concatenate(
                [vt_ref[h * A_HEAD_DIM:(h + 1) * A_HEAD_DIM, blk(kb)], ones], axis=0)
            acc_scr[h] = alpha_scr[h] * acc_scr[h] + _dot(v1, pr)

    def att_body(kb, c, didx):
        att_block(kb, didx)
        return c

    n_far = jnp.maximum(i - 1, 0)
    lax.fori_loop(0, n_far, functools.partial(att_body, didx=None), 0)
    lax.fori_loop(n_far, i, functools.partial(att_body, didx=1), 0)
    att_block(i, 0)
    score_keys(ikn_ref, iwtn_ref, i1, nxt, True)
    for h in range(A_HEADS):
        acc = acc_scr[h]
        ot_scr[h * A_HEAD_DIM:(h + 1) * A_HEAD_DIM, :] = (
            acc[:A_HEAD_DIM] * (1.0 / acc[A_HEAD_DIM:A_HEAD_DIM + 1]))
    o_ref[...] = jnp.transpose(ot_scr[...]).astype(BF16)


def _dsa(q, k, vt, iq, ik2, iwt, bias, bsz, seq):
    t = DSA_TILE
    nq = seq // t
    top_k = min(TOPK_MAX, seq // 4)
    assert t <= top_k and t % CHUNK == 0 and seq % t == 0
    nsteps = bsz * nq
    nxt = lambda s: jnp.minimum(s + 1, nsteps - 1)
    iq_w, ik_w = IDX_HEADS * IDX_DIM, 2 * IDX_DIM
    return pl.pallas_call(
        functools.partial(_dsa_kernel, top_k=top_k, nq=nq, nsteps=nsteps),
        grid=(nsteps,),
        in_specs=[pl.BlockSpec((t, A_WIDTH), lambda s: (s, 0)),
                  pl.BlockSpec((seq, A_WIDTH), lambda s: (s // nq, 0)),
                  pl.BlockSpec((A_WIDTH, seq), lambda s: (s // nq, 0)),
                  pl.BlockSpec((t, iq_w), lambda s: (0, 0)),
                  pl.BlockSpec((seq, ik_w), lambda s: (0, 0)),
                  pl.BlockSpec((IDX_HEADS, t), lambda s: (0, 0)),
                  pl.BlockSpec((t, iq_w), lambda s: (nxt(s), 0)),
                  pl.BlockSpec((seq, ik_w), lambda s: (nxt(s) // nq, 0)),
                  pl.BlockSpec((IDX_HEADS, t), lambda s: (0, nxt(s))),
                  _resident(bias.shape)],
        out_specs=pl.BlockSpec((t, A_WIDTH), lambda s: (s, 0)),
        out_shape=jax.ShapeDtypeStruct((bsz * seq, A_WIDTH), BF16),
        scratch_shapes=[pltpu.VMEM((2, seq, t), F32), pltpu.VMEM((seq, t), F32),
                        pltpu.VMEM((IDX_HEADS, t, LANES), BF16),
                        pltpu.VMEM((A_HEADS, t, LANES), BF16),
                        pltpu.VMEM((A_WIDTH, t), F32),
                        pltpu.VMEM((A_HEADS, 1, t), F32),
                        pltpu.VMEM((A_HEADS, 1, t), F32),
                        pltpu.VMEM((A_HEADS, A_HEAD_DIM + SUM_ROWS, t), F32),
                        pltpu.VMEM((A_HEADS, t, t), F32)],
        compiler_params=_params(("arbitrary",)),
        name="dsa",
    )(q, k, vt, iq, ik2, iwt, iq, ik2, iwt, bias)


HALO = 16


def _ffn_kernel(*refs, n_y, tiles_per_seq, final_norm):
    x_ref, xh_ref = refs[0], refs[1]
    y_refs = refs[2:2 + 2 * n_y]
    (wo_ref, g_ref, wup_ref, cw_ref, cb_ref, wdn_ref, gfin_ref, o_ref,
     ua_scr, ub_scr, act_scr, acc_scr) = refs[2 + 2 * n_y:]
    tm = x_ref.shape[0]

    half = tm // 2
    halves = (slice(0, half), slice(half, tm))

    def mixed(x_r, ys, rows):
        acc = x_r[rows]
        r0 = 0
        for y_r in ys:
            n = y_r.shape[1]
            acc = acc + _dot(y_r[rows], wo_ref[r0:r0 + n, :])
            r0 += n
        return acc

    xh = mixed(xh_ref, y_refs[1::2], slice(0, HALO))
    xs = [mixed(x_ref, y_refs[0::2], r) for r in halves]
    keep = jnp.where(pl.program_id(0) % tiles_per_seq == 0, 0.0, 1.0)
    hh = (_rms(xh, g_ref[...]) * keep).astype(BF16)

    def conv(scr, c0, rows):
        w = cw_ref[:, c0:c0 + FF_COLS]
        r0, r1 = HALO + rows.start, HALO + rows.stop
        return (cb_ref[:, c0:c0 + FF_COLS]
                + w[2:3] * scr[r0:r1]
                + w[1:2] * scr[r0 - 1:r1 - 1]
                + w[0:1] * scr[r0 - 2:r1 - 2])

    def gate(j, rows):
        ca, cb = j * FF_COLS, D_FF + j * FF_COLS
        act_scr[j % 2, rows] = (jax.nn.silu(conv(ua_scr.at[j % 2], ca, rows))
                                * conv(ub_scr.at[j % 2], cb, rows)).astype(BF16)

    nj = D_FF // FF_COLS

    def up(j, hs, rows=slice(None)):
        ua_scr[j % 2, rows] = _dot(hs, wup_ref[:, j * FF_COLS:(j + 1) * FF_COLS])
        ub_scr[j % 2, rows] = _dot(hs, wup_ref[:, D_FF + j * FF_COLS:D_FF + (j + 1) * FF_COLS])

    def down(j, rows=slice(None)):
        return _dot(act_scr[j % 2, rows], wdn_ref[j * FF_COLS:(j + 1) * FF_COLS, :])

    hn0 = _rms(xs[0], g_ref[...]).astype(BF16)
    up(0, jnp.concatenate([hh, hn0], axis=0), slice(0, half + HALO))
    hn1 = _rms(xs[1], g_ref[...]).astype(BF16)
    up(0, jnp.concatenate([hn0[half - HALO:], hn1], axis=0), slice(half, tm + HALO))
    he = jnp.concatenate([hh, hn0, hn1], axis=0)

    for j in range(nj):
        if j + 1 < nj:
            up(j + 1, he)
        if j == 1:
            acc_scr[...] = down(0)
        elif j > 1:
            acc_scr[...] += down(j - 1)
        if j + 1 < nj:
            gate(j, slice(0, tm))
    for r, x in zip(halves, xs):
        gate(nj - 1, r)
        acc_scr[r] += down(nj - 1, r)
        y = x + acc_scr[r]
        if final_norm:
            y = _rms(y, gfin_ref[...])
        o_ref[r] = y


def _ffn(x2, ys, wo, g, wup, cw, cb, wdn, gfin, seq, final_norm):
    m = x2.shape[0]
    tm = ROW_TILE
    assert seq % tm == 0 and D_FF % FF_COLS == 0 and tm % HALO == 0
    row = lambda n: pl.BlockSpec((tm, n), lambda i: (i, 0))
    halo = lambda n: pl.BlockSpec((HALO, n), lambda i: (jnp.maximum(i * (tm // HALO) - 1, 0), 0))
    y_specs, y_args = [], []
    for y in ys:
        y_specs += [row(y.shape[1]), halo(y.shape[1])]
        y_args += [y, y]
    consts = (wo, g, wup, cw, cb, wdn, gfin)
    return pl.pallas_call(
        functools.partial(_ffn_kernel, n_y=len(ys), tiles_per_seq=seq // tm,
                          final_norm=final_norm),
        grid=(m // tm,),
        in_specs=[row(D_MODEL), halo(D_MODEL)] + y_specs + [_resident(c.shape) for c in consts],
        out_specs=row(D_MODEL),
        out_shape=jax.ShapeDtypeStruct((m, D_MODEL), F32),
        scratch_shapes=[pltpu.VMEM((2, tm + HALO, FF_COLS), F32),
                        pltpu.VMEM((2, tm + HALO, FF_COLS), F32),
                        pltpu.VMEM((2, tm, FF_COLS), BF16),
                        pltpu.VMEM((tm, D_MODEL), F32)],
        compiler_params=_params(("parallel",)),
        name="conv_ffn",
    )(x2, x2, *y_args, *consts)


def _hgrn_kernel(x_ref, g_ref, w_ref, lbp_ref, gout_ref, o_ref, proj_scr, st_scr, *, layer):
    ts = x_ref.shape[0]
    kw = C_HEADS * C_KEY_DIM
    c = CHUNK
    nsub = c // HG_SUB

    @pl.when(pl.program_id(1) == 0)
    def _():
        st_scr[...] = jnp.zeros(st_scr.shape, F32)

    h = _rms(x_ref[...], g_ref[...]).astype(BF16)

    def project(c0, c1):
        pw = 512
        for n in range(c0, c1, pw):
            proj_scr[:, n:n + pw] = _dot(h, w_ref[:, n:n + pw])

    lbp = lbp_ref[...]
    e = jnp.exp(lbp - jnp.max(lbp, axis=0, keepdims=True))
    sm = e / jnp.sum(e, axis=0, keepdims=True)
    lb = jnp.sum(sm[1:layer + 1], axis=0, keepdims=True) if layer > 0 else jnp.zeros((1, kw), F32)

    ri = lax.broadcasted_iota(I32, (c, c), 0)
    ci = lax.broadcasted_iota(I32, (c, c), 1)
    r3 = lax.broadcasted_iota(I32, (c, 3 * c), 0)
    c3 = lax.broadcasted_iota(I32, (c, 3 * c), 1) % c
    tri3 = jnp.where(r3 >= c3, 1.0, 0.0).astype(BF16)
    dmask = (ri >= ci) & (ri // HG_SUB == ci // HG_SUB)

    def gates(rows):
        gg = lb + (1.0 - lb) * jax.nn.sigmoid(proj_scr[rows, kw:2 * kw])
        lg = jnp.log(gg)
        hi = lg.astype(BF16)
        r1 = lg - hi.astype(F32)
        mid = r1.astype(BF16)
        lo = (r1 - mid.astype(F32)).astype(BF16)
        return gg, jnp.concatenate([hi, mid, lo], axis=0)

    def decay_products(rows, gg, b_all):
        out = []
        for hd in range(C_HEADS):
            kc = slice(hd * C_KEY_DIM, (hd + 1) * C_KEY_DIM)
            q = jax.nn.silu(proj_scr[rows, hd * C_KEY_DIM:(hd + 1) * C_KEY_DIM])
            kk = 1.0 - gg[:, kc]
            b = b_all[:, kc]
            qs, ks = [], []
            for j in range(nsub - 1):
                e0, e1 = j * HG_SUB, (j + 1) * HG_SUB
                rj = b[e1 - 1:e1]
                qs.append(jnp.concatenate(
                    [jnp.zeros((e1, C_KEY_DIM), F32), q[e1:] * jnp.exp(b[e1:] - rj)], axis=0))
                kj = kk[e0:e1] * jnp.exp(rj - b[e0:e1])
                parts = [kj, jnp.zeros((c - e1, C_KEY_DIM), F32)]
                if e0:
                    parts = [jnp.zeros((e0, C_KEY_DIM), F32)] + parts
                ks.append(jnp.concatenate(parts, axis=0))
            a_off = _dot_nt(jnp.concatenate(qs, axis=1).astype(BF16),
                            jnp.concatenate(ks, axis=1).astype(BF16))
            bm = jnp.concatenate(
                [jnp.broadcast_to(b[j * HG_SUB + HG_SUB // 2:j * HG_SUB + HG_SUB // 2 + 1],
                                  (HG_SUB, C_KEY_DIM)) for j in range(nsub)], axis=0)
            a_dg = _dot_nt((q * jnp.exp(b - bm)).astype(BF16), (kk * jnp.exp(bm - b)).astype(BF16))
            a = (a_off + jnp.where(dmask, a_dg, 0.0)).astype(BF16)
            bl = b[c - 1:c]
            out.append([(q * jnp.exp(b)).astype(BF16), a,
                        (kk * jnp.exp(bl - b)).astype(BF16), jnp.exp(bl)])
        return out

    def values(rows, hd):
        return proj_scr[rows, 2 * kw + hd * C_VAL_DIM:2 * kw + (hd + 1) * C_VAL_DIM].astype(BF16)

    def back(rows, parts, intras):
        sts = [st_scr[hd] for hd in range(C_HEADS)]
        inters = [_dot_nt(p[0], st.astype(BF16)) for p, st in zip(parts, sts)]
        upds = [_dot_tn(values(rows, hd), p[2]) for hd, p in enumerate(parts)]
        for hd, p in enumerate(parts):
            st_scr[hd] = sts[hd] * p[3] + upds[hd]
            gate = proj_scr[rows, 2 * kw + C_WIDTH + hd * C_VAL_DIM:
                            2 * kw + C_WIDTH + (hd + 1) * C_VAL_DIM]
            o_ref[rows, hd * C_VAL_DIM:(hd + 1) * C_VAL_DIM] = (
                _rms(inters[hd] + intras[hd], gout_ref[...]) * jax.nn.silu(gate)).astype(BF16)

    rows = [slice(n * c, (n + 1) * c) for n in range(ts // c)]
    project(kw, 2 * kw)
    gate_parts = [gates(r) for r in rows]
    project(0, kw)
    cums = [_dot(tri3, lg3) for _, lg3 in gate_parts]
    project(2 * kw, 2 * kw + C_WIDTH)
    parts = [decay_products(r, gg, b_all) for r, (gg, _), b_all in zip(rows, gate_parts, cums)]
    project(2 * kw + C_WIDTH, 2 * kw + 2 * C_WIDTH)
    intras = [[_dot(p[1], values(r, hd)) for hd, p in enumerate(ps)]
              for r, ps in zip(rows, parts)]
    for r, ps, it in zip(rows, parts, intras):
        back(r, ps, it)


def _hgrn(x2, g, w, lbp, gout, bsz, seq, layer):
    ts = HG_TILE
    nt = seq // ts
    assert seq % ts == 0 and ts % CHUNK == 0 and CHUNK % HG_SUB == 0
    row = lambda n: pl.BlockSpec((ts, n), lambda b, i: (b * nt + i, 0))
    return pl.pallas_call(
        functools.partial(_hgrn_kernel, layer=layer),
        grid=(bsz, nt),
        in_specs=[row(D_MODEL), _resident(g.shape), _resident(w.shape), _resident(lbp.shape),
                  _resident(gout.shape)],
        out_specs=row(C_WIDTH),
        out_shape=jax.ShapeDtypeStruct((bsz * seq, C_WIDTH), BF16),
        scratch_shapes=[pltpu.VMEM((ts, w.shape[1]), F32),
                        pltpu.VMEM((C_HEADS, C_VAL_DIM, C_KEY_DIM), F32)],
        compiler_params=_params(("parallel", "arbitrary")),
        name="hgrn2",
    )(x2, g, w, lbp, gout)


def kernel(x, rel_bias, hgrn_lb, mix_norm, ffn_norm, final_norm, ab_w_in, ab_idx_k_norm, ab_gmlp_norm, ab_w_s, ab_b_s, ab_w_out, c_w_in, c_out_norm, c_w_out, ffn_w_up, ffn_conv_w, ffn_conv_b, ffn_w_down):
    bsz, seq, d = x.shape
    depth = mix_norm.shape[0]
    x2 = x.reshape(bsz * seq, d)
    row = lambda a: a.reshape(1, -1)
    bias = _bias_tiles(rel_bias)
    for l in range(depth):
        if l % 2 == 0:
            e = l // 2
            w = ab_w_in[e]
            offs = [0]
            for s in (A_WIDTH, A_WIDTH, A_WIDTH, IDX_HEADS * IDX_DIM, IDX_DIM, IDX_HEADS,
                      B_WIDTH, B_WIDTH):
                offs.append(offs[-1] + s)
            seg = [w[:, offs[n]:offs[n + 1]] for n in range(8)]
            w_p = jnp.concatenate(
                seg[0:4] + seg[6:8] + [seg[4], seg[4], seg[5],
                                       jnp.zeros((d, LANES - IDX_HEADS), w.dtype)],
                axis=1).astype(BF16)
            assert w_p.shape[1] == _C_END
            gik = jnp.concatenate([ab_idx_k_norm[e], ab_idx_k_norm[e]])
            q, k, v, iq, ik2, iw, y_b = _proj_ab(
                x2, row(mix_norm[l]), w_p, row(gik), row(ab_gmlp_norm[e]),
                ab_w_s[e], jnp.transpose(ab_b_s[e]))
            vt = jnp.transpose(v.reshape(bsz, seq, A_WIDTH), (0, 2, 1)).reshape(bsz * A_WIDTH, seq)
            y_a = _dsa(q, k, vt, iq, ik2, jnp.transpose(iw), bias, bsz, seq)
            ys, wo = [y_a, y_b], ab_w_out[e]
        else:
            o_i = l // 2
            og = _hgrn(x2, row(mix_norm[l]), c_w_in[o_i].astype(BF16), hgrn_lb,
                       row(c_out_norm[o_i]), bsz, seq, l)
            ys, wo = [og], c_w_out[o_i]
        x2 = _ffn(x2, ys, wo.astype(BF16), row(ffn_norm[l]), ffn_w_up[l].astype(BF16),
                  ffn_conv_w[l], row(ffn_conv_b[l]), ffn_w_down[l].astype(BF16),
                  row(final_norm), seq, l == depth - 1)
    return x2.reshape(bsz, seq, d)
```

```python
import functools
import math

import jax
import jax.numpy as jnp
from jax import lax
from jax.experimental import pallas as pl
from jax.experimental.pallas import tpu as pltpu

F32 = jnp.float32
BF16 = jnp.bfloat16
I32 = jnp.int32

D_MODEL = 1024
CHUNK = 64
A_HEADS = 8
A_HEAD_DIM = 64
A_WIDTH = A_HEADS * A_HEAD_DIM
IDX_HEADS = 16
IDX_DIM = 64
TOPK_MAX = 256
REL_BUCKETS = 32
REL_MAX_DIST = 128
B_GROUPS = 8
B_GROUP_DIM = 64
B_WIDTH = B_GROUPS * B_GROUP_DIM
B_CHUNK = 128
C_HEADS = 8
C_KEY_DIM = 128
C_VAL_DIM = 128
C_WIDTH = C_HEADS * C_VAL_DIM
D_FF = 2816
EPS = 1e-6

LANES = 128
VMEM_LIMIT = 56 * 1024 * 1024

ROW_TILE = 512
DSA_TILE = 256
HG_TILE = 256
HG_SUB = 16
FF_COLS = 256
NEG = -1e30
M_INIT = -1e29
INT_MIN = -2 ** 31
SEARCH_HEAD_BITS = 24
SUM_ROWS = 16

_C_Q, _C_K, _C_V, _C_IQ, _C_END = 0, 512, 1024, 1536, 2560
_T_U, _T_VG, _T_IK, _T_IW, _T_END = 0, 512, 1024, 1152, 1280


def _params(sem):
    return pltpu.CompilerParams(dimension_semantics=sem, vmem_limit_bytes=VMEM_LIMIT)


def _resident(shape):
    nd = len(shape)
    return pl.BlockSpec(shape, lambda *_: (0,) * nd, pipeline_mode=pl.Buffered(1))


def _rms(x, g):
    ms = jnp.mean(x * x, axis=-1, keepdims=True)
    return x * lax.rsqrt(ms + EPS) * g


def _dot(a, b):
    return jnp.dot(a, b, preferred_element_type=F32)


def _dot_nt(a, b):
    return lax.dot_general(a, b, (((1,), (1,)), ((), ())), preferred_element_type=F32)


def _dot_tn(a, b):
    return lax.dot_general(a, b, (((0,), (0,)), ((), ())), preferred_element_type=F32)


def _t5_bucket(rel):
    half = REL_BUCKETS // 2
    max_exact = half // 2
    ret = jnp.where(rel > 0, half, 0)
    n = jnp.abs(rel)
    nf = jnp.maximum(n, max_exact).astype(F32)
    large = max_exact + (jnp.log(nf / max_exact) / math.log(REL_MAX_DIST / max_exact)
                         * (half - max_exact)).astype(I32)
    large = jnp.minimum(large, half - 1)
    return ret + jnp.where(n < max_exact, n, large)


def _bias_kernel(row_ref, o_ref):
    t = DSA_TILE
    x = jnp.broadcast_to(row_ref[0, 0], (t, 2 * t))
    o_ref[0, 0] = pltpu.roll(x, 0, 1, stride=1, stride_axis=0)[:, :t]


def _bias_tiles(rel_bias):
    t = DSA_TILE
    assert t >= REL_MAX_DIST
    m = jnp.arange(2 * t, dtype=I32)
    qk = jnp.where(m < t, m, m - 2 * t)
    rel = jnp.stack([-qk, -qk - t])
    table = rel_bias[_t5_bucket(rel)] - rel_bias[_t5_bucket(jnp.int32(-2 * t))]
    rows = jnp.transpose(table, (0, 2, 1)).reshape(2, A_HEADS, 1, 2 * t)
    return pl.pallas_call(
        _bias_kernel,
        grid=(2, A_HEADS),
        in_specs=[pl.BlockSpec((1, 1, 1, 2 * t), lambda d, h: (d, h, 0, 0))],
        out_specs=pl.BlockSpec((1, 1, t, t), lambda d, h: (d, h, 0, 0)),
        out_shape=jax.ShapeDtypeStruct((2, A_HEADS, t, t), F32),
        compiler_params=_params(("arbitrary", "arbitrary")),
        name="bias_tiles",
    )(rows)


def _proj_ab_kernel(x_ref, g_ref, w_ref, wt_ref, gik_ref, ggm_ref, ws_ref, bs_ref,
                    q_ref, k_ref, vt_ref, iq_ref, ik_ref, iwt_ref, yb_ref):
    tm = x_ref.shape[0]
    h = _rms(x_ref[...], g_ref[...]).astype(BF16)

    def proj(c0, c1):
        return _dot(h, w_ref[:, c0:c1])

    def tail(c0, c1):
        return _dot(h, wt_ref[:, c0:c1])

    q_ref[...] = (proj(_C_Q, _C_K) * (A_HEAD_DIM ** -0.5)).astype(BF16)
    k_ref[...] = proj(_C_K, _C_V).astype(BF16)
    vt_ref[...] = jnp.transpose(proj(_C_V, _C_IQ)).astype(BF16)
    half = (_C_END - _C_IQ) // 2
    for j in range(2):
        iq_ref[:, j * half:(j + 1) * half] = (
            proj(_C_IQ + j * half, _C_IQ + (j + 1) * half) * (IDX_DIM ** -0.5)).astype(BF16)
    ik_ref[...] = _rms(tail(_T_IK, _T_IW), gik_ref[...]).astype(BF16)
    iwt_ref[...] = jnp.transpose(tail(_T_IW, _T_END))[:IDX_HEADS] * (IDX_HEADS ** -0.5)

    u = jax.nn.gelu(tail(_T_U, _T_VG))
    vn = _rms(jax.nn.gelu(tail(_T_VG, _T_IK)), ggm_ref[...]).astype(BF16)
    ri = lax.broadcasted_iota(I32, (B_CHUNK, B_CHUNK), 0) // CHUNK
    ci = lax.broadcasted_iota(I32, (B_CHUNK, B_CHUNK), 1) // CHUNK
    low = ri >= ci
    first = lax.broadcasted_iota(I32, (B_CHUNK, LANES), 1) < B_GROUP_DIM
    for p in range(B_GROUPS // 2):
        w_e = jnp.where(low, ws_ref[2 * p], 0.0).astype(BF16)
        w_o = jnp.where(low, ws_ref[2 * p + 1], 0.0).astype(BF16)
        b_e = bs_ref[:, 2 * p:2 * p + 1]
        b_o = bs_ref[:, 2 * p + 1:2 * p + 2]
        cols = slice(p * LANES, (p + 1) * LANES)
        for r in range(tm // B_CHUNK):
            rows = slice(r * B_CHUNK, (r + 1) * B_CHUNK)
            vp = vn[rows, cols]
            s = jnp.where(first, _dot(w_e, vp) + b_e, _dot(w_o, vp) + b_o)
            yb_ref[rows, cols] = (u[rows, cols] * s).astype(BF16)


def _proj_ab(x2, g, w, wt, gik, ggm, ws, bs_t, seq):
    m = x2.shape[0]
    tm = ROW_TILE
    tps = seq // tm
    row = lambda n: pl.BlockSpec((tm, n), lambda i: (i, 0))
    sds = jax.ShapeDtypeStruct
    consts = (g, w, wt, gik, ggm, ws, bs_t)
    return pl.pallas_call(
        _proj_ab_kernel,
        grid=(m // tm,),
        in_specs=[row(D_MODEL)] + [_resident(c.shape) for c in consts],
        out_specs=[row(A_WIDTH), row(A_WIDTH),
                   pl.BlockSpec((A_WIDTH, tm), lambda i: (i // tps, i % tps)),
                   row(IDX_HEADS * IDX_DIM), row(2 * IDX_DIM),
                   pl.BlockSpec((IDX_HEADS, tm), lambda i: (0, i)), row(B_WIDTH)],
        out_shape=[sds((m, A_WIDTH), BF16), sds((m, A_WIDTH), BF16),
                   sds((m // seq * A_WIDTH, seq), BF16), sds((m, IDX_HEADS * IDX_DIM), BF16),
                   sds((m, 2 * IDX_DIM), BF16), sds((IDX_HEADS, m), F32),
                   sds((m, B_WIDTH), BF16)],
        compiler_params=_params(("parallel",)),
        name="proj_ab",
    )(x2, *consts)


def _dsa_kernel(q_ref, k_ref, vt_ref, iq0_ref, ik0_ref, iwt0_ref, iqn_ref, ikn_ref, iwtn_ref,
                bias_ref, o_ref,
                keys_scr, madd_scr, iqm_scr, qm_scr, ot_scr, m_scr, alpha_scr, acc_scr, s_scr,
                *, top_k, nq, nsteps):
    t = DSA_TILE
    sub = 8
    step = pl.program_id(0)
    i = step % nq
    i1 = jnp.where(step + 1 < nsteps, (step + 1) % nq, 0)
    cur = step % 2
    nxt = 1 - cur
    first = lax.broadcasted_iota(I32, (t, LANES), 1) < A_HEAD_DIM
    allowed = (lax.broadcasted_iota(I32, (t, t), 0) // CHUNK
               <= lax.broadcasted_iota(I32, (t, t), 1) // CHUNK)

    def blk(kb):
        return pl.ds(pl.multiple_of(kb * t, t), t)

    def split_heads(src_ref, dst_scr, heads):
        for h in range(heads):
            pair = src_ref[:, (h // 2) * LANES:(h // 2 + 1) * LANES]
            dst_scr[h] = jnp.where(first if h % 2 == 0 else ~first, pair, jnp.zeros_like(pair))

    def score_keys(ik_ref, iwt_ref, kb, slot, diag):
        ikb = ik_ref[blk(kb), :]
        acc = jnp.zeros((t, t), F32)
        for h in range(IDX_HEADS):
            s = _dot_nt(ikb, iqm_scr[h])
            acc = acc + jnp.maximum(s, 0.0) * iwt_ref[h:h + 1, :]
        if diag:
            acc = jnp.where(allowed, acc, -jnp.inf)
        keys_scr[slot, blk(kb), :] = acc

    @pl.when(step == 0)
    def _():
        split_heads(iq0_ref, iqm_scr, IDX_HEADS)
        score_keys(ik0_ref, iwt0_ref, 0, 0, True)

    split_heads(iqn_ref, iqm_scr, IDX_HEADS)
    split_heads(q_ref, qm_scr, A_HEADS)

    @pl.when(i == 0)
    def _():
        madd_scr[0:t, :] = jnp.where(allowed, 0.0, NEG)

    @pl.when(i > 0)
    def _():
        def count(cmp, thr):
            nacc = 4

            def body(kb, cnts):
                kk = keys_scr[cur, blk(kb), :]
                cnts = list(cnts)
                for r in range(t // sub):
                    cnts[r % nacc] = cnts[r % nacc] + jnp.where(
                        cmp(kk[r * sub:(r + 1) * sub], thr), 1.0, 0.0)
                return tuple(cnts)
            cnts = lax.fori_loop(0, i + 1, body, (jnp.zeros((sub, t), F32),) * nacc)
            return jnp.sum(sum(cnts), axis=0, keepdims=True)

        ge = lambda a, b: a >= b
        gt = lambda a, b: a > b

        def as_score(key):
            bits = key ^ ((key >> 31) & jnp.int32(0x7FFFFFFF))
            return lax.bitcast_convert_type(bits, F32)

        def bit_body(it, c):
            key, n_ge = c
            cand = key + (jnp.int32(1) << (31 - it))
            cnt = count(ge, as_score(cand))
            ok = cnt >= top_k
            return jnp.where(ok, cand, key), jnp.where(ok, cnt, n_ge)

        def unsettled(n_ge):
            return jnp.max(jnp.abs(n_ge - top_k))

        def tail_body(c):
            it, _, key, n_ge = c
            key, n_ge = bit_body(it, (key, n_ge))
            return it + 1, unsettled(n_ge), key, n_ge

        key, n_ge = lax.fori_loop(
            0, SEARCH_HEAD_BITS, bit_body,
            (jnp.full((sub, t), INT_MIN, I32), jnp.full((1, t), jnp.inf, F32)))
        _, _, key, n_ge = lax.while_loop(
            lambda c: (c[0] < 32) & (c[1] > 0), tail_body,
            (jnp.int32(SEARCH_HEAD_BITS), unsettled(n_ge), key, n_ge))
        thr = as_score(key)
        thr_row = thr[0:1]

        def plain_body(kb, c):
            madd_scr[blk(kb), :] = jnp.where(keys_scr[cur, blk(kb), :] >= thr_row, 0.0, NEG)
            return c

        lax.fori_loop(0, i + 1, plain_body, 0)

        @pl.when(jnp.max(n_ge) > top_k)
        def _():
            need = top_k - count(gt, thr)
            lower = (lax.broadcasted_iota(I32, (t, t), 1)
                     < lax.broadcasted_iota(I32, (t, t), 0))
            lower = jnp.where(lower, 1.0, 0.0).astype(BF16)

            def tie_body(kb, run):
                kk = keys_scr[cur, blk(kb), :]
                eq = jnp.where(kk == thr_row, 1.0, 0.0)
                rank = run + _dot(lower, eq.astype(BF16))
                sel = (kk > thr_row) | ((kk == thr_row) & (rank < need))
                madd_scr[blk(kb), :] = jnp.where(sel, 0.0, NEG)
                return run + jnp.sum(eq, axis=0, keepdims=True)

            lax.fori_loop(0, i + 1, tie_body, jnp.zeros((1, t), F32))

    ones = jnp.ones((SUM_ROWS, t), BF16)

    m_scr[...] = jnp.full(m_scr.shape, M_INIT, F32)
    acc_scr[...] = jnp.zeros(acc_scr.shape, F32)

    def att_block(kb, didx):
        madd = madd_scr[blk(kb), :]
        for h in range(A_HEADS):
            cols = slice((h // 2) * LANES, (h // 2 + 1) * LANES)
            s = _dot_nt(k_ref[blk(kb), cols], qm_scr[h]) + madd
            if didx is not None:
                s = s + bias_ref[didx, h]
            s_scr[h] = s
            m = m_scr[h]
            m_new = jnp.maximum(m, jnp.max(s, axis=0, keepdims=True))
            alpha_scr[h] = jnp.exp(m - m_new)
            m_scr[h] = m_new

        score_keys(ikn_ref, iwtn_ref, kb, nxt, False)

        for h in range(A_HEADS):
            pr = jnp.exp(s_scr[h] - m_scr[h]).astype(BF16)
            v1 = jnp.concatenate(
                [vt_ref[h * A_HEAD_DIM:(h + 1) * A_HEAD_DIM, blk(kb)], ones], axis=0)
            acc_scr[h] = alpha_scr[h] * acc_scr[h] + _dot(v1, pr)

    def att_body(kb, c, didx):
        att_block(kb, didx)
        return c

    n_far = jnp.maximum(i - 1, 0)
    lax.fori_loop(0, n_far, functools.partial(att_body, didx=None), 0)
    lax.fori_loop(n_far, i, functools.partial(att_body, didx=1), 0)
    att_block(i, 0)
    score_keys(ikn_ref, iwtn_ref, i1, nxt, True)
    for h in range(A_HEADS):
        acc = acc_scr[h]
        ot_scr[h * A_HEAD_DIM:(h + 1) * A_HEAD_DIM, :] = (
            acc[:A_HEAD_DIM] * (1.0 / acc[A_HEAD_DIM:A_HEAD_DIM + 1]))
    o_ref[...] = jnp.transpose(ot_scr[...]).astype(BF16)


def _dsa(q, k, vt, iq, ik2, iwt, bias, bsz, seq):
    t = DSA_TILE
    nq = seq // t
    top_k = min(TOPK_MAX, seq // 4)
    assert t <= top_k and t % CHUNK == 0 and seq % t == 0
    nsteps = bsz * nq
    nxt = lambda s: jnp.minimum(s + 1, nsteps - 1)
    iq_w, ik_w = IDX_HEADS * IDX_DIM, 2 * IDX_DIM
    return pl.pallas_call(
        functools.partial(_dsa_kernel, top_k=top_k, nq=nq, nsteps=nsteps),
        grid=(nsteps,),
        in_specs=[pl.BlockSpec((t, A_WIDTH), lambda s: (s, 0)),
                  pl.BlockSpec((seq, A_WIDTH), lambda s: (s // nq, 0)),
                  pl.BlockSpec((A_WIDTH, seq), lambda s: (s // nq, 0)),
                  pl.BlockSpec((t, iq_w), lambda s: (0, 0)),
                  pl.BlockSpec((seq, ik_w), lambda s: (0, 0)),
                  pl.BlockSpec((IDX_HEADS, t), lambda s: (0, 0)),
                  pl.BlockSpec((t, iq_w), lambda s: (nxt(s), 0)),
                  pl.BlockSpec((seq, ik_w), lambda s: (nxt(s) // nq, 0)),
                  pl.BlockSpec((IDX_HEADS, t), lambda s: (0, nxt(s))),
                  _resident(bias.shape)],
        out_specs=pl.BlockSpec((t, A_WIDTH), lambda s: (s, 0)),
        out_shape=jax.ShapeDtypeStruct((bsz * seq, A_WIDTH), BF16),
        scratch_shapes=[pltpu.VMEM((2, seq, t), F32), pltpu.VMEM((seq, t), F32),
                        pltpu.VMEM((IDX_HEADS, t, LANES), BF16),
                        pltpu.VMEM((A_HEADS, t, LANES), BF16),
                        pltpu.VMEM((A_WIDTH, t), F32),
                        pltpu.VMEM((A_HEADS, 1, t), F32),
                        pltpu.VMEM((A_HEADS, 1, t), F32),
                        pltpu.VMEM((A_HEADS, A_HEAD_DIM + SUM_ROWS, t), F32),
                        pltpu.VMEM((A_HEADS, t, t), F32)],
        compiler_params=_params(("arbitrary",)),
        name="dsa",
    )(q, k, vt, iq, ik2, iwt, iq, ik2, iwt, bias)


HALO = 16


def _ffn_kernel(*refs, n_y, tiles_per_seq, final_norm):
    x_ref, xh_ref = refs[0], refs[1]
    y_refs = refs[2:2 + 2 * n_y]
    (wo_ref, g_ref, wup_ref, cw_ref, cb_ref, wdn_ref, gfin_ref, o_ref,
     ua_scr, ub_scr, act_scr, acc_scr) = refs[2 + 2 * n_y:]
    tm = x_ref.shape[0]

    half = tm // 2
    halves = (slice(0, half), slice(half, tm))

    def mixed(x_r, ys, rows):
        acc = x_r[rows]
        r0 = 0
        for y_r in ys:
            n = y_r.shape[1]
            acc = acc + _dot(y_r[rows], wo_ref[r0:r0 + n, :])
            r0 += n
        return acc

    xh = mixed(xh_ref, y_refs[1::2], slice(0, HALO))
    xs = [mixed(x_ref, y_refs[0::2], r) for r in halves]
    keep = jnp.where(pl.program_id(0) % tiles_per_seq == 0, 0.0, 1.0)
    hh = (_rms(xh, g_ref[...]) * keep).astype(BF16)

    def conv(scr, c0, rows):
        w = cw_ref[:, c0:c0 + FF_COLS]
        r0, r1 = HALO + rows.start, HALO + rows.stop
        return (cb_ref[:, c0:c0 + FF_COLS]
                + w[2:3] * scr[r0:r1]
                + w[1:2] * scr[r0 - 1:r1 - 1]
                + w[0:1] * scr[r0 - 2:r1 - 2])

    def gate(j, rows):
        ca, cb = j * FF_COLS, D_FF + j * FF_COLS
        act_scr[j % 2, rows] = (jax.nn.silu(conv(ua_scr.at[j % 2], ca, rows))
                                * conv(ub_scr.at[j % 2], cb, rows)).astype(BF16)

    nj = D_FF // FF_COLS

    def up(j, hs, rows=slice(None)):
        ua_scr[j % 2, rows] = _dot(hs, wup_ref[:, j * FF_COLS:(j + 1) * FF_COLS])
        ub_scr[j % 2, rows] = _dot(hs, wup_ref[:, D_FF + j * FF_COLS:D_FF + (j + 1) * FF_COLS])

    def down(j, rows=slice(None)):
        return _dot(act_scr[j % 2, rows], wdn_ref[j * FF_COLS:(j + 1) * FF_COLS, :])

    hn0 = _rms(xs[0], g_ref[...]).astype(BF16)
    up(0, jnp.concatenate([hh, hn0], axis=0), slice(0, half + HALO))
    hn1 = _rms(xs[1], g_ref[...]).astype(BF16)
    up(0, jnp.concatenate([hn0[half - HALO:], hn1], axis=0), slice(half, tm + HALO))
    he = jnp.concatenate([hh, hn0, hn1], axis=0)

    for j in range(nj):
        if j + 1 < nj:
            up(j + 1, he)
        if j == 1:
            acc_scr[...] = down(0)
        elif j > 1:
            acc_scr[...] += down(j - 1)
        if j + 1 < nj:
            gate(j, slice(0, tm))
    for r, x in zip(halves, xs):
        gate(nj - 1, r)
        acc_scr[r] += down(nj - 1, r)
        y = x + acc_scr[r]
        if final_norm:
            y = _rms(y, gfin_ref[...])
        o_ref[r] = y


def _ffn(x2, ys, wo, g, wup, cw, cb, wdn, gfin, seq, final_norm):
    m = x2.shape[0]
    tm = ROW_TILE
    assert seq % tm == 0 and D_FF % FF_COLS == 0 and tm % HALO == 0
    row = lambda n: pl.BlockSpec((tm, n), lambda i: (i, 0))
    halo = lambda n: pl.BlockSpec((HALO, n), lambda i: (jnp.maximum(i * (tm // HALO) - 1, 0), 0))
    y_specs, y_args = [], []
    for y in ys:
        y_specs += [row(y.shape[1]), halo(y.shape[1])]
        y_args += [y, y]
    consts = (wo, g, wup, cw, cb, wdn, gfin)
    return pl.pallas_call(
        functools.partial(_ffn_kernel, n_y=len(ys), tiles_per_seq=seq // tm,
                          final_norm=final_norm),
        grid=(m // tm,),
        in_specs=[row(D_MODEL), halo(D_MODEL)] + y_specs + [_resident(c.shape) for c in consts],
        out_specs=row(D_MODEL),
        out_shape=jax.ShapeDtypeStruct((m, D_MODEL), F32),
        scratch_shapes=[pltpu.VMEM((2, tm + HALO, FF_COLS), F32),
                        pltpu.VMEM((2, tm + HALO, FF_COLS), F32),
                        pltpu.VMEM((2, tm, FF_COLS), BF16),
                        pltpu.VMEM((tm, D_MODEL), F32)],
        compiler_params=_params(("parallel",)),
        name="conv_ffn",
    )(x2, x2, *y_args, *consts)


def _hgrn_kernel(x_ref, g_ref, w_ref, lbp_ref, gout_ref, o_ref, proj_scr, st_scr, *, layer):
    ts = x_ref.shape[0]
    kw = C_HEADS * C_KEY_DIM
    c = CHUNK
    nsub = c // HG_SUB

    @pl.when(pl.program_id(1) == 0)
    def _():
        st_scr[...] = jnp.zeros(st_scr.shape, F32)

    h = _rms(x_ref[...], g_ref[...]).astype(BF16)

    def project(c0, c1):
        pw = 512
        for n in range(c0, c1, pw):
            proj_scr[:, n:n + pw] = _dot(h, w_ref[:, n:n + pw])

    lbp = lbp_ref[...]
    e = jnp.exp(lbp - jnp.max(lbp, axis=0, keepdims=True))
    sm = e / jnp.sum(e, axis=0, keepdims=True)
    lb = jnp.sum(sm[1:layer + 1], axis=0, keepdims=True) if layer > 0 else jnp.zeros((1, kw), F32)

    ri = lax.broadcasted_iota(I32, (c, c), 0)
    ci = lax.broadcasted_iota(I32, (c, c), 1)
    r3 = lax.broadcasted_iota(I32, (c, 3 * c), 0)
    c3 = lax.broadcasted_iota(I32, (c, 3 * c), 1) % c
    tri3 = jnp.where(r3 >= c3, 1.0, 0.0).astype(BF16)
    dmask = (ri >= ci) & (ri // HG_SUB == ci // HG_SUB)

    def gates(rows):
        gg = lb + (1.0 - lb) * jax.nn.sigmoid(proj_scr[rows, kw:2 * kw])
        lg = jnp.log(gg)
        hi = lg.astype(BF16)
        r1 = lg - hi.astype(F32)
        mid = r1.astype(BF16)
        lo = (r1 - mid.astype(F32)).astype(BF16)
        return gg, jnp.concatenate([hi, mid, lo], axis=0)

    def decay_products(rows, gg, b_all):
        out = []
        for hd in range(C_HEADS):
            kc = slice(hd * C_KEY_DIM, (hd + 1) * C_KEY_DIM)
            q = jax.nn.silu(proj_scr[rows, hd * C_KEY_DIM:(hd + 1) * C_KEY_DIM])
            kk = 1.0 - gg[:, kc]
            b = b_all[:, kc]
            qs, ks = [], []
            for j in range(nsub - 1):
                e0, e1 = j * HG_SUB, (j + 1) * HG_SUB
                rj = b[e1 - 1:e1]
                qs.append(jnp.concatenate(
                    [jnp.zeros((e1, C_KEY_DIM), F32), q[e1:] * jnp.exp(b[e1:] - rj)], axis=0))
                kj = kk[e0:e1] * jnp.exp(rj - b[e0:e1])
                parts = [kj, jnp.zeros((c - e1, C_KEY_DIM), F32)]
                if e0:
                    parts = [jnp.zeros((e0, C_KEY_DIM), F32)] + parts
                ks.append(jnp.concatenate(parts, axis=0))
            a_off = _dot_nt(jnp.concatenate(qs, axis=1).astype(BF16),
                            jnp.concatenate(ks, axis=1).astype(BF16))
            bm = jnp.concatenate(
                [jnp.broadcast_to(b[j * HG_SUB + HG_SUB // 2:j * HG_SUB + HG_SUB // 2 + 1],
                                  (HG_SUB, C_KEY_DIM)) for j in range(nsub)], axis=0)
            a_dg = _dot_nt((q * jnp.exp(b - bm)).astype(BF16), (kk * jnp.exp(bm - b)).astype(BF16))
            a = (a_off + jnp.where(dmask, a_dg, 0.0)).astype(BF16)
            bl = b[c - 1:c]
            out.append([(q * jnp.exp(b)).astype(BF16), a,
                        (kk * jnp.exp(bl - b)).astype(BF16), jnp.exp(bl)])
        return out

    def values(rows, hd):
        return proj_scr[rows, 2 * kw + hd * C_VAL_DIM:2 * kw + (hd + 1) * C_VAL_DIM].astype(BF16)

    def back(rows, parts, intras):
        sts = [st_scr[hd] for hd in range(C_HEADS)]
        inters = [_dot_nt(p[0], st.astype(BF16)) for p, st in zip(parts, sts)]
        upds = [_dot_tn(values(rows, hd), p[2]) for hd, p in enumerate(parts)]
        for hd, p in enumerate(parts):
            st_scr[hd] = sts[hd] * p[3] + upds[hd]
            gate = proj_scr[rows, 2 * kw + C_WIDTH + hd * C_VAL_DIM:
                            2 * kw + C_WIDTH + (hd + 1) * C_VAL_DIM]
            o_ref[rows, hd * C_VAL_DIM:(hd + 1) * C_VAL_DIM] = (
                _rms(inters[hd] + intras[hd], gout_ref[...]) * jax.nn.silu(gate)).astype(BF16)

    rows = [slice(n * c, (n + 1) * c) for n in range(ts // c)]
    project(kw, 2 * kw)
    gate_parts = [gates(r) for r in rows]
    project(0, kw)
    cums = [_dot(tri3, lg3) for _, lg3 in gate_parts]
    project(2 * kw, 2 * kw + C_WIDTH)
    parts = [decay_products(r, gg, b_all) for r, (gg, _), b_all in zip(rows, gate_parts, cums)]
    project(2 * kw + C_WIDTH, 2 * kw + 2 * C_WIDTH)
    intras = [[_dot(p[1], values(r, hd)) for hd, p in enumerate(ps)]
              for r, ps in zip(rows, parts)]
    for r, ps, it in zip(rows, parts, intras):
        back(r, ps, it)


def _hgrn(x2, g, w, lbp, gout, bsz, seq, layer):
    ts = HG_TILE
    nt = seq // ts
    assert seq % ts == 0 and ts % CHUNK == 0 and CHUNK % HG_SUB == 0
    row = lambda n: pl.BlockSpec((ts, n), lambda b, i: (b * nt + i, 0))
    return pl.pallas_call(
        functools.partial(_hgrn_kernel, layer=layer),
        grid=(bsz, nt),
        in_specs=[row(D_MODEL), _resident(g.shape), _resident(w.shape), _resident(lbp.shape),
                  _resident(gout.shape)],
        out_specs=row(C_WIDTH),
        out_shape=jax.ShapeDtypeStruct((bsz * seq, C_WIDTH), BF16),
        scratch_shapes=[pltpu.VMEM((ts, w.shape[1]), F32),
                        pltpu.VMEM((C_HEADS, C_VAL_DIM, C_KEY_DIM), F32)],
        compiler_params=_params(("parallel", "arbitrary")),
        name="hgrn2",
    )(x2, g, w, lbp, gout)


def kernel(x, rel_bias, hgrn_lb, mix_norm, ffn_norm, final_norm, ab_w_in, ab_idx_k_norm, ab_gmlp_norm, ab_w_s, ab_b_s, ab_w_out, c_w_in, c_out_norm, c_w_out, ffn_w_up, ffn_conv_w, ffn_conv_b, ffn_w_down):
    bsz, seq, d = x.shape
    depth = mix_norm.shape[0]
    x2 = x.reshape(bsz * seq, d)
    row = lambda a: a.reshape(1, -1)
    bias = _bias_tiles(rel_bias)
    for l in range(depth):
        if l % 2 == 0:
            e = l // 2
            w = ab_w_in[e]
            offs = [0]
            for s in (A_WIDTH, A_WIDTH, A_WIDTH, IDX_HEADS * IDX_DIM, IDX_DIM, IDX_HEADS,
                      B_WIDTH, B_WIDTH):
                offs.append(offs[-1] + s)
            seg = [w[:, offs[n]:offs[n + 1]] for n in range(8)]
            assert offs[4] == _C_END
            w_head = w[:, :_C_END].astype(BF16)
            w_tail = jnp.concatenate(
                [seg[6], seg[7], seg[4], seg[4], seg[5],
                 jnp.zeros((d, LANES - IDX_HEADS), w.dtype)], axis=1).astype(BF16)
            assert w_tail.shape[1] == _T_END
            gik = jnp.concatenate([ab_idx_k_norm[e], ab_idx_k_norm[e]])
            q, k, vt, iq, ik2, iwt, y_b = _proj_ab(
                x2, row(mix_norm[l]), w_head, w_tail, row(gik), row(ab_gmlp_norm[e]),
                ab_w_s[e], jnp.transpose(ab_b_s[e]), seq)
            y_a = _dsa(q, k, vt, iq, ik2, iwt, bias, bsz, seq)
            ys, wo = [y_a, y_b], ab_w_out[e]
        else:
            o_i = l // 2
            og = _hgrn(x2, row(mix_norm[l]), c_w_in[o_i].astype(BF16), hgrn_lb,
                       row(c_out_norm[o_i]), bsz, seq, l)
            ys, wo = [og], c_w_out[o_i]
        x2 = _ffn(x2, ys, wo.astype(BF16), row(ffn_norm[l]), ffn_w_up[l].astype(BF16),
                  ffn_conv_w[l], row(ffn_conv_b[l]), ffn_w_down[l].astype(BF16),
                  row(final_norm), seq, l == depth - 1)
    return x2.reshape(bsz, seq, d)
```

```python
import functools
import math

import jax
import jax.numpy as jnp
from jax import lax
from jax.experimental import pallas as pl
from jax.experimental.pallas import tpu as pltpu

F32 = jnp.float32
BF16 = jnp.bfloat16
I32 = jnp.int32

D_MODEL = 1024
CHUNK = 64
A_HEADS = 8
A_HEAD_DIM = 64
A_WIDTH = A_HEADS * A_HEAD_DIM
IDX_HEADS = 16
IDX_DIM = 64
TOPK_MAX = 256
REL_BUCKETS = 32
REL_MAX_DIST = 128
B_GROUPS = 8
B_GROUP_DIM = 64
B_WIDTH = B_GROUPS * B_GROUP_DIM
B_CHUNK = 128
C_HEADS = 8
C_KEY_DIM = 128
C_VAL_DIM = 128
C_WIDTH = C_HEADS * C_VAL_DIM
D_FF = 2816
EPS = 1e-6

LANES = 128
VMEM_LIMIT = 56 * 1024 * 1024

ROW_TILE = 512
DSA_TILE = 256
HG_TILE = 512
HG_SUB = 16
FF_COLS = 256
NEG = -1e30
M_INIT = -1e29
INT_MIN = -2 ** 31
SEARCH_HEAD_BITS = 24
SUM_ROWS = 16

_C_Q, _C_K, _C_V, _C_IQ, _C_END = 0, 512, 1024, 1536, 2560
_T_U, _T_VG, _T_IK, _T_IW, _T_END = 0, 512, 1024, 1152, 1280


def _params(sem):
    return pltpu.CompilerParams(dimension_semantics=sem, vmem_limit_bytes=VMEM_LIMIT)


def _resident(shape):
    nd = len(shape)
    return pl.BlockSpec(shape, lambda *_: (0,) * nd, pipeline_mode=pl.Buffered(1))


def _rms(x, g):
    ms = jnp.mean(x * x, axis=-1, keepdims=True)
    return x * lax.rsqrt(ms + EPS) * g


def _dot(a, b):
    return jnp.dot(a, b, preferred_element_type=F32)


def _dot_nt(a, b):
    return lax.dot_general(a, b, (((1,), (1,)), ((), ())), preferred_element_type=F32)


def _dot_tn(a, b):
    return lax.dot_general(a, b, (((0,), (0,)), ((), ())), preferred_element_type=F32)


def _t5_bucket(rel):
    half = REL_BUCKETS // 2
    max_exact = half // 2
    ret = jnp.where(rel > 0, half, 0)
    n = jnp.abs(rel)
    nf = jnp.maximum(n, max_exact).astype(F32)
    large = max_exact + (jnp.log(nf / max_exact) / math.log(REL_MAX_DIST / max_exact)
                         * (half - max_exact)).astype(I32)
    large = jnp.minimum(large, half - 1)
    return ret + jnp.where(n < max_exact, n, large)


def _bias_kernel(row_ref, o_ref):
    t = DSA_TILE
    x = jnp.broadcast_to(row_ref[0, 0], (t, 2 * t))
    o_ref[0, 0] = pltpu.roll(x, 0, 1, stride=1, stride_axis=0)[:, :t]


def _bias_tiles(rel_bias):
    t = DSA_TILE
    assert t >= REL_MAX_DIST
    m = jnp.arange(2 * t, dtype=I32)
    qk = jnp.where(m < t, m, m - 2 * t)
    rel = jnp.stack([-qk, -qk - t])
    table = rel_bias[_t5_bucket(rel)] - rel_bias[_t5_bucket(jnp.int32(-2 * t))]
    rows = jnp.transpose(table, (0, 2, 1)).reshape(2, A_HEADS, 1, 2 * t)
    return pl.pallas_call(
        _bias_kernel,
        grid=(2, A_HEADS),
        in_specs=[pl.BlockSpec((1, 1, 1, 2 * t), lambda d, h: (d, h, 0, 0))],
        out_specs=pl.BlockSpec((1, 1, t, t), lambda d, h: (d, h, 0, 0)),
        out_shape=jax.ShapeDtypeStruct((2, A_HEADS, t, t), F32),
        compiler_params=_params(("arbitrary", "arbitrary")),
        name="bias_tiles",
    )(rows)


def _proj_ab_kernel(x_ref, g_ref, w_ref, wt_ref, gik_ref, ggm_ref, ws_ref, bs_ref,
                    q_ref, k_ref, vt_ref, iq_ref, ik_ref, iwt_ref, yb_ref):
    tm = x_ref.shape[0]
    h = _rms(x_ref[...], g_ref[...]).astype(BF16)

    def proj(c0, c1):
        return _dot(h, w_ref[:, c0:c1])

    def tail(c0, c1):
        return _dot(h, wt_ref[:, c0:c1])

    q_ref[...] = (proj(_C_Q, _C_K) * (A_HEAD_DIM ** -0.5)).astype(BF16)
    k_ref[...] = proj(_C_K, _C_V).astype(BF16)
    vt_ref[...] = jnp.transpose(proj(_C_V, _C_IQ)).astype(BF16)
    half = (_C_END - _C_IQ) // 2
    for j in range(2):
        iq_ref[:, j * half:(j + 1) * half] = (
            proj(_C_IQ + j * half, _C_IQ + (j + 1) * half) * (IDX_DIM ** -0.5)).astype(BF16)
    ik_ref[...] = _rms(tail(_T_IK, _T_IW), gik_ref[...]).astype(BF16)
    iwt_ref[...] = jnp.transpose(tail(_T_IW, _T_END))[:IDX_HEADS] * (IDX_HEADS ** -0.5)

    u = jax.nn.gelu(tail(_T_U, _T_VG))
    vn = _rms(jax.nn.gelu(tail(_T_VG, _T_IK)), ggm_ref[...]).astype(BF16)
    ri = lax.broadcasted_iota(I32, (B_CHUNK, B_CHUNK), 0) // CHUNK
    ci = lax.broadcasted_iota(I32, (B_CHUNK, B_CHUNK), 1) // CHUNK
    low = ri >= ci
    first = lax.broadcasted_iota(I32, (B_CHUNK, LANES), 1) < B_GROUP_DIM
    for p in range(B_GROUPS // 2):
        w_e = jnp.where(low, ws_ref[2 * p], 0.0).astype(BF16)
        w_o = jnp.where(low, ws_ref[2 * p + 1], 0.0).astype(BF16)
        b_e = bs_ref[:, 2 * p:2 * p + 1]
        b_o = bs_ref[:, 2 * p + 1:2 * p + 2]
        cols = slice(p * LANES, (p + 1) * LANES)
        for r in range(tm // B_CHUNK):
            rows = slice(r * B_CHUNK, (r + 1) * B_CHUNK)
            vp = vn[rows, cols]
            s = jnp.where(first, _dot(w_e, vp) + b_e, _dot(w_o, vp) + b_o)
            yb_ref[rows, cols] = (u[rows, cols] * s).astype(BF16)


def _proj_ab(x2, g, w, wt, gik, ggm, ws, bs_t, seq):
    m = x2.shape[0]
    tm = ROW_TILE
    tps = seq // tm
    row = lambda n: pl.BlockSpec((tm, n), lambda i: (i, 0))
    sds = jax.ShapeDtypeStruct
    consts = (g, w, wt, gik, ggm, ws, bs_t)
    return pl.pallas_call(
        _proj_ab_kernel,
        grid=(m // tm,),
        in_specs=[row(D_MODEL)] + [_resident(c.shape) for c in consts],
        out_specs=[row(A_WIDTH), row(A_WIDTH),
                   pl.BlockSpec((A_WIDTH, tm), lambda i: (i // tps, i % tps)),
                   row(IDX_HEADS * IDX_DIM), row(2 * IDX_DIM),
                   pl.BlockSpec((IDX_HEADS, tm), lambda i: (0, i)), row(B_WIDTH)],
        out_shape=[sds((m, A_WIDTH), BF16), sds((m, A_WIDTH), BF16),
                   sds((m // seq * A_WIDTH, seq), BF16), sds((m, IDX_HEADS * IDX_DIM), BF16),
                   sds((m, 2 * IDX_DIM), BF16), sds((IDX_HEADS, m), F32),
                   sds((m, B_WIDTH), BF16)],
        compiler_params=_params(("parallel",)),
        name="proj_ab",
    )(x2, *consts)


def _dsa_kernel(q_ref, k_ref, vt_ref, iq0_ref, ik0_ref, iwt0_ref, iqn_ref, ikn_ref, iwtn_ref,
                bias_ref, o_ref,
                keys_scr, madd_scr, iqm_scr, qm_scr, ot_scr, m_scr, alpha_scr, acc_scr, s_scr,
                *, top_k, nq, nsteps):
    t = DSA_TILE
    sub = 8
    step = pl.program_id(0)
    i = step % nq
    i1 = jnp.where(step + 1 < nsteps, (step + 1) % nq, 0)
    cur = step % 2
    nxt = 1 - cur
    first = lax.broadcasted_iota(I32, (t, LANES), 1) < A_HEAD_DIM
    allowed = (lax.broadcasted_iota(I32, (t, t), 0) // CHUNK
               <= lax.broadcasted_iota(I32, (t, t), 1) // CHUNK)

    def blk(kb):
        return pl.ds(pl.multiple_of(kb * t, t), t)

    def split_heads(src_ref, dst_scr, heads):
        for h in range(heads):
            pair = src_ref[:, (h // 2) * LANES:(h // 2 + 1) * LANES]
            dst_scr[h] = jnp.where(first if h % 2 == 0 else ~first, pair, jnp.zeros_like(pair))

    def score_keys(ik_ref, iwt_ref, kb, slot, diag):
        ikb = ik_ref[blk(kb), :]
        acc = jnp.zeros((t, t), F32)
        for h in range(IDX_HEADS):
            s = _dot_nt(ikb, iqm_scr[h])
            acc = acc + jnp.maximum(s, 0.0) * iwt_ref[h:h + 1, :]
        if diag:
            acc = jnp.where(allowed, acc, -jnp.inf)
        keys_scr[slot, blk(kb), :] = acc

    @pl.when(step == 0)
    def _():
        split_heads(iq0_ref, iqm_scr, IDX_HEADS)
        score_keys(ik0_ref, iwt0_ref, 0, 0, True)

    split_heads(iqn_ref, iqm_scr, IDX_HEADS)
    split_heads(q_ref, qm_scr, A_HEADS)

    @pl.when(i == 0)
    def _():
        madd_scr[0:t, :] = jnp.where(allowed, 0.0, NEG)

    @pl.when(i > 0)
    def _():
        def count(cmp, thr):
            nacc = 4

            def body(kb, cnts):
                kk = keys_scr[cur, blk(kb), :]
                cnts = list(cnts)
                for r in range(t // sub):
                    cnts[r % nacc] = cnts[r % nacc] + jnp.where(
                        cmp(kk[r * sub:(r + 1) * sub], thr), 1.0, 0.0)
                return tuple(cnts)
            cnts = lax.fori_loop(0, i + 1, body, (jnp.zeros((sub, t), F32),) * nacc)
            return jnp.sum(sum(cnts), axis=0, keepdims=True)

        ge = lambda a, b: a >= b
        gt = lambda a, b: a > b

        def as_score(key):
            bits = key ^ ((key >> 31) & jnp.int32(0x7FFFFFFF))
            return lax.bitcast_convert_type(bits, F32)

        def bit_body(it, c):
            key, n_ge = c
            cand = key + (jnp.int32(1) << (31 - it))
            cnt = count(ge, as_score(cand))
            ok = cnt >= top_k
            return jnp.where(ok, cand, key), jnp.where(ok, cnt, n_ge)

        def unsettled(n_ge):
            return jnp.max(jnp.abs(n_ge - top_k))

        def tail_body(c):
            it, _, key, n_ge = c
            key, n_ge = bit_body(it, (key, n_ge))
            return it + 1, unsettled(n_ge), key, n_ge

        key, n_ge = lax.fori_loop(
            0, SEARCH_HEAD_BITS, bit_body,
            (jnp.full((sub, t), INT_MIN, I32), jnp.full((1, t), jnp.inf, F32)))
        _, _, key, n_ge = lax.while_loop(
            lambda c: (c[0] < 32) & (c[1] > 0), tail_body,
            (jnp.int32(SEARCH_HEAD_BITS), unsettled(n_ge), key, n_ge))
        thr = as_score(key)
        thr_row = thr[0:1]

        def plain_body(kb, c):
            madd_scr[blk(kb), :] = jnp.where(keys_scr[cur, blk(kb), :] >= thr_row, 0.0, NEG)
            return c

        lax.fori_loop(0, i + 1, plain_body, 0)

        @pl.when(jnp.max(n_ge) > top_k)
        def _():
            need = top_k - count(gt, thr)
            lower = (lax.broadcasted_iota(I32, (t, t), 1)
                     < lax.broadcasted_iota(I32, (t, t), 0))
            lower = jnp.where(lower, 1.0, 0.0).astype(BF16)

            def tie_body(kb, run):
                kk = keys_scr[cur, blk(kb), :]
                eq = jnp.where(kk == thr_row, 1.0, 0.0)
                rank = run + _dot(lower, eq.astype(BF16))
                sel = (kk > thr_row) | ((kk == thr_row) & (rank < need))
                madd_scr[blk(kb), :] = jnp.where(sel, 0.0, NEG)
                return run + jnp.sum(eq, axis=0, keepdims=True)

            lax.fori_loop(0, i + 1, tie_body, jnp.zeros((1, t), F32))

    ones = jnp.ones((SUM_ROWS, t), BF16)

    m_scr[...] = jnp.full(m_scr.shape, M_INIT, F32)
    acc_scr[...] = jnp.zeros(acc_scr.shape, F32)

    def att_block(kb, didx, score_next):
        madd = madd_scr[blk(kb), :]
        for h in range(A_HEADS):
            cols = slice((h // 2) * LANES, (h // 2 + 1) * LANES)
            s = _dot_nt(k_ref[blk(kb), cols], qm_scr[h]) + madd
            if didx is not None:
                s = s + bias_ref[didx, h]
            s_scr[h] = s
            m = m_scr[h]
            m_new = jnp.maximum(m, jnp.max(s, axis=0, keepdims=True))
            alpha_scr[h] = jnp.exp(m - m_new)
            m_scr[h] = m_new

        if score_next:
            score_keys(ikn_ref, iwtn_ref, kb, nxt, False)

        for h in range(A_HEADS):
            pr = jnp.exp(s_scr[h] - m_scr[h]).astype(BF16)
            v1 = jnp.concatenate(
                [vt_ref[h * A_HEAD_DIM:(h + 1) * A_HEAD_DIM, blk(kb)], ones], axis=0)
            acc_scr[h] = alpha_scr[h] * acc_scr[h] + _dot(v1, pr)

    def att_body(kb, c, didx, score_next):
        att_block(kb, didx, score_next)
        return c

    same_seq = i1 > 0
    n_far = jnp.maximum(i - 1, 0)
    for score_next in (True, False):
        on = same_seq if score_next else jnp.logical_not(same_seq)
        lax.fori_loop(0, jnp.where(on, n_far, 0),
                      functools.partial(att_body, didx=None, score_next=score_next), 0)
    for score_next in (True, False):
        on = same_seq if score_next else jnp.logical_not(same_seq)
        lax.fori_loop(n_far, jnp.where(on, i, n_far),
                      functools.partial(att_body, didx=1, score_next=score_next), 0)
    for score_next in (True, False):
        on = same_seq if score_next else jnp.logical_not(same_seq)
        lax.fori_loop(0, jnp.where(on, 1, 0),
                      lambda _, c, sn=score_next: att_body(i, c, 0, sn), 0)
    score_keys(ikn_ref, iwtn_ref, i1, nxt, True)
    for h in range(A_HEADS):
        acc = acc_scr[h]
        ot_scr[h * A_HEAD_DIM:(h + 1) * A_HEAD_DIM, :] = (
            acc[:A_HEAD_DIM] * (1.0 / acc[A_HEAD_DIM:A_HEAD_DIM + 1]))
    o_ref[...] = jnp.transpose(ot_scr[...]).astype(BF16)


def _dsa(q, k, vt, iq, ik2, iwt, bias, bsz, seq):
    t = DSA_TILE
    nq = seq // t
    top_k = min(TOPK_MAX, seq // 4)
    assert t <= top_k and t % CHUNK == 0 and seq % t == 0
    nsteps = bsz * nq
    nxt = lambda s: jnp.minimum(s + 1, nsteps - 1)
    iq_w, ik_w = IDX_HEADS * IDX_DIM, 2 * IDX_DIM
    return pl.pallas_call(
        functools.partial(_dsa_kernel, top_k=top_k, nq=nq, nsteps=nsteps),
        grid=(nsteps,),
        in_specs=[pl.BlockSpec((t, A_WIDTH), lambda s: (s, 0)),
                  pl.BlockSpec((seq, A_WIDTH), lambda s: (s // nq, 0)),
                  pl.BlockSpec((A_WIDTH, seq), lambda s: (s // nq, 0)),
                  pl.BlockSpec((t, iq_w), lambda s: (0, 0)),
                  pl.BlockSpec((seq, ik_w), lambda s: (0, 0)),
                  pl.BlockSpec((IDX_HEADS, t), lambda s: (0, 0)),
                  pl.BlockSpec((t, iq_w), lambda s: (nxt(s), 0)),
                  pl.BlockSpec((seq, ik_w), lambda s: (nxt(s) // nq, 0)),
                  pl.BlockSpec((IDX_HEADS, t), lambda s: (0, nxt(s))),
                  _resident(bias.shape)],
        out_specs=pl.BlockSpec((t, A_WIDTH), lambda s: (s, 0)),
        out_shape=jax.ShapeDtypeStruct((bsz * seq, A_WIDTH), BF16),
        scratch_shapes=[pltpu.VMEM((2, seq, t), F32), pltpu.VMEM((seq, t), F32),
                        pltpu.VMEM((IDX_HEADS, t, LANES), BF16),
                        pltpu.VMEM((A_HEADS, t, LANES), BF16),
                        pltpu.VMEM((A_WIDTH, t), F32),
                        pltpu.VMEM((A_HEADS, 1, t), F32),
                        pltpu.VMEM((A_HEADS, 1, t), F32),
                        pltpu.VMEM((A_HEADS, A_HEAD_DIM + SUM_ROWS, t), F32),
                        pltpu.VMEM((A_HEADS, t, t), F32)],
        compiler_params=_params(("arbitrary",)),
        name="dsa",
    )(q, k, vt, iq, ik2, iwt, iq, ik2, iwt, bias)


HALO = 16


def _ffn_kernel(*refs, n_y, tiles_per_seq, final_norm):
    x_ref, xh_ref = refs[0], refs[1]
    y_refs = refs[2:2 + 2 * n_y]
    (wo_ref, g_ref, wup_ref, cw_ref, cb_ref, wdn_ref, gfin_ref, o_ref,
     ua_scr, ub_scr, act_scr, acc_scr) = refs[2 + 2 * n_y:]
    tm = x_ref.shape[0]

    half = tm // 2
    halves = (slice(0, half), slice(half, tm))

    def mixed(x_r, ys, rows):
        acc = x_r[rows]
        r0 = 0
        for y_r in ys:
            n = y_r.shape[1]
            acc = acc + _dot(y_r[rows], wo_ref[r0:r0 + n, :])
            r0 += n
        return acc

    xh = mixed(xh_ref, y_refs[1::2], slice(0, HALO))
    xs = [mixed(x_ref, y_refs[0::2], r) for r in halves]
    keep = jnp.where(pl.program_id(0) % tiles_per_seq == 0, 0.0, 1.0)
    hh = (_rms(xh, g_ref[...]) * keep).astype(BF16)

    def conv(scr, c0, rows):
        w = cw_ref[:, c0:c0 + FF_COLS]
        r0, r1 = HALO + rows.start, HALO + rows.stop
        return (cb_ref[:, c0:c0 + FF_COLS]
                + w[2:3] * scr[r0:r1]
                + w[1:2] * scr[r0 - 1:r1 - 1]
                + w[0:1] * scr[r0 - 2:r1 - 2])

    def gate(j, rows):
        ca, cb = j * FF_COLS, D_FF + j * FF_COLS
        act_scr[j % 2, rows] = (jax.nn.silu(conv(ua_scr.at[j % 2], ca, rows))
                                * conv(ub_scr.at[j % 2], cb, rows)).astype(BF16)

    nj = D_FF // FF_COLS

    def up(j, hs, rows=slice(None)):
        ua_scr[j % 2, rows] = _dot(hs, wup_ref[:, j * FF_COLS:(j + 1) * FF_COLS])
        ub_scr[j % 2, rows] = _dot(hs, wup_ref[:, D_FF + j * FF_COLS:D_FF + (j + 1) * FF_COLS])

    def down(j, rows=slice(None)):
        return _dot(act_scr[j % 2, rows], wdn_ref[j * FF_COLS:(j + 1) * FF_COLS, :])

    hn0 = _rms(xs[0], g_ref[...]).astype(BF16)
    up(0, jnp.concatenate([hh, hn0], axis=0), slice(0, half + HALO))
    hn1 = _rms(xs[1], g_ref[...]).astype(BF16)
    up(0, jnp.concatenate([hn0[half - HALO:], hn1], axis=0), slice(half, tm + HALO))
    he = jnp.concatenate([hh, hn0, hn1], axis=0)

    for j in range(nj):
        if j + 1 < nj:
            up(j + 1, he)
        if j == 1:
            acc_scr[...] = down(0)
        elif j > 1:
            acc_scr[...] += down(j - 1)
        if j + 1 < nj:
            gate(j, slice(0, tm))
    for r, x in zip(halves, xs):
        gate(nj - 1, r)
        acc_scr[r] += down(nj - 1, r)
        y = x + acc_scr[r]
        if final_norm:
            y = _rms(y, gfin_ref[...])
        o_ref[r] = y


def _layer_resident(a, layer):
    nd = a.ndim - 1
    return pl.BlockSpec((None,) + a.shape[1:], lambda *_: (layer,) + (0,) * nd,
                        pipeline_mode=pl.Buffered(1))


def _ffn(x2, ys, wo, g, wup, cw, cb, wdn, gfin, seq, layer, final_norm):
    m = x2.shape[0]
    tm = ROW_TILE
    assert seq % tm == 0 and D_FF % FF_COLS == 0 and tm % HALO == 0
    row = lambda n: pl.BlockSpec((tm, n), lambda i: (i, 0))
    halo = lambda n: pl.BlockSpec((HALO, n), lambda i: (jnp.maximum(i * (tm // HALO) - 1, 0), 0))
    y_specs, y_args = [], []
    for y in ys:
        y_specs += [row(y.shape[1]), halo(y.shape[1])]
        y_args += [y, y]
    consts = (wo, g, wup, cw, cb, wdn, gfin)
    const_specs = [_layer_resident(c, layer) if c is wup or c is cw or c is cb or c is wdn
                   else _resident(c.shape) for c in consts]
    return pl.pallas_call(
        functools.partial(_ffn_kernel, n_y=len(ys), tiles_per_seq=seq // tm,
                          final_norm=final_norm),
        grid=(m // tm,),
        in_specs=[row(D_MODEL), halo(D_MODEL)] + y_specs + const_specs,
        out_specs=row(D_MODEL),
        out_shape=jax.ShapeDtypeStruct((m, D_MODEL), F32),
        scratch_shapes=[pltpu.VMEM((2, tm + HALO, FF_COLS), F32),
                        pltpu.VMEM((2, tm + HALO, FF_COLS), F32),
                        pltpu.VMEM((2, tm, FF_COLS), BF16),
                        pltpu.VMEM((tm, D_MODEL), F32)],
        compiler_params=_params(("parallel",)),
        name="conv_ffn",
    )(x2, x2, *y_args, *consts)


def _hgrn_kernel(x_ref, g_ref, w_ref, lbp_ref, gout_ref, o_ref, proj_scr, st_scr, *, layer):
    ts = x_ref.shape[0]
    kw = C_HEADS * C_KEY_DIM
    c = CHUNK
    nsub = c // HG_SUB

    @pl.when(pl.program_id(1) == 0)
    def _():
        st_scr[...] = jnp.zeros(st_scr.shape, F32)

    h = _rms(x_ref[...], g_ref[...]).astype(BF16)

    def project(c0, c1):
        pw = 512
        for n in range(c0, c1, pw):
            proj_scr[:, n:n + pw] = _dot(h, w_ref[:, n:n + pw])

    lbp = lbp_ref[...]
    e = jnp.exp(lbp - jnp.max(lbp, axis=0, keepdims=True))
    sm = e / jnp.sum(e, axis=0, keepdims=True)
    lb = jnp.sum(sm[1:layer + 1], axis=0, keepdims=True) if layer > 0 else jnp.zeros((1, kw), F32)

    ri = lax.broadcasted_iota(I32, (c, c), 0)
    ci = lax.broadcasted_iota(I32, (c, c), 1)
    r3 = lax.broadcasted_iota(I32, (c, 3 * c), 0)
    c3 = lax.broadcasted_iota(I32, (c, 3 * c), 1) % c
    tri3 = jnp.where(r3 >= c3, 1.0, 0.0).astype(BF16)
    dmask = (ri >= ci) & (ri // HG_SUB == ci // HG_SUB)

    def gates(rows):
        gg = lb + (1.0 - lb) * jax.nn.sigmoid(proj_scr[rows, kw:2 * kw])
        lg = jnp.log(gg)
        hi = lg.astype(BF16)
        r1 = lg - hi.astype(F32)
        mid = r1.astype(BF16)
        lo = (r1 - mid.astype(F32)).astype(BF16)
        return gg, jnp.concatenate([hi, mid, lo], axis=0)

    def decay_products(rows, gg, b_all):
        out = []
        for hd in range(C_HEADS):
            kc = slice(hd * C_KEY_DIM, (hd + 1) * C_KEY_DIM)
            q = jax.nn.silu(proj_scr[rows, hd * C_KEY_DIM:(hd + 1) * C_KEY_DIM])
            kk = 1.0 - gg[:, kc]
            b = b_all[:, kc]
            qs, ks = [], []
            for j in range(nsub - 1):
                e0, e1 = j * HG_SUB, (j + 1) * HG_SUB
                rj = b[e1 - 1:e1]
                qs.append(jnp.concatenate(
                    [jnp.zeros((e1, C_KEY_DIM), F32), q[e1:] * jnp.exp(b[e1:] - rj)], axis=0))
                kj = kk[e0:e1] * jnp.exp(rj - b[e0:e1])
                parts = [kj, jnp.zeros((c - e1, C_KEY_DIM), F32)]
                if e0:
                    parts = [jnp.zeros((e0, C_KEY_DIM), F32)] + parts
                ks.append(jnp.concatenate(parts, axis=0))
            a_off = _dot_nt(jnp.concatenate(qs, axis=1).astype(BF16),
                            jnp.concatenate(ks, axis=1).astype(BF16))
            bm = jnp.concatenate(
                [jnp.broadcast_to(b[j * HG_SUB + HG_SUB // 2:j * HG_SUB + HG_SUB // 2 + 1],
                                  (HG_SUB, C_KEY_DIM)) for j in range(nsub)], axis=0)
            a_dg = _dot_nt((q * jnp.exp(b - bm)).astype(BF16), (kk * jnp.exp(bm - b)).astype(BF16))
            a = (a_off + jnp.where(dmask, a_dg, 0.0)).astype(BF16)
            bl = b[c - 1:c]
            out.append([(q * jnp.exp(b)).astype(BF16), a,
                        (kk * jnp.exp(bl - b)).astype(BF16), jnp.exp(bl)])
        return out

    def values(rows, hd):
        return proj_scr[rows, 2 * kw + hd * C_VAL_DIM:2 * kw + (hd + 1) * C_VAL_DIM].astype(BF16)

    def back(rows, parts, intras):
        sts = [st_scr[hd] for hd in range(C_HEADS)]
        inters = [_dot_nt(p[0], st.astype(BF16)) for p, st in zip(parts, sts)]
        upds = [_dot_tn(values(rows, hd), p[2]) for hd, p in enumerate(parts)]
        for hd, p in enumerate(parts):
            st_scr[hd] = sts[hd] * p[3] + upds[hd]
            gate = proj_scr[rows, 2 * kw + C_WIDTH + hd * C_VAL_DIM:
                            2 * kw + C_WIDTH + (hd + 1) * C_VAL_DIM]
            o_ref[rows, hd * C_VAL_DIM:(hd + 1) * C_VAL_DIM] = (
                _rms(inters[hd] + intras[hd], gout_ref[...]) * jax.nn.silu(gate)).astype(BF16)

    rows = [slice(n * c, (n + 1) * c) for n in range(ts // c)]
    project(kw, 2 * kw)
    gate_parts = [gates(r) for r in rows]
    project(0, kw)
    cums = [_dot(tri3, lg3) for _, lg3 in gate_parts]
    project(2 * kw, 2 * kw + C_WIDTH)
    parts = [decay_products(r, gg, b_all) for r, (gg, _), b_all in zip(rows, gate_parts, cums)]
    project(2 * kw + C_WIDTH, 2 * kw + 2 * C_WIDTH)
    intras = [[_dot(p[1], values(r, hd)) for hd, p in enumerate(ps)]
              for r, ps in zip(rows, parts)]
    for r, ps, it in zip(rows, parts, intras):
        back(r, ps, it)


def _hgrn(x2, g, w, lbp, gout, bsz, seq, layer):
    ts = HG_TILE
    nt = seq // ts
    assert seq % ts == 0 and ts % CHUNK == 0 and CHUNK % HG_SUB == 0
    row = lambda n: pl.BlockSpec((ts, n), lambda b, i: (b * nt + i, 0))
    return pl.pallas_call(
        functools.partial(_hgrn_kernel, layer=layer),
        grid=(bsz, nt),
        in_specs=[row(D_MODEL), _resident(g.shape), _resident(w.shape), _resident(lbp.shape),
                  _resident(gout.shape)],
        out_specs=row(C_WIDTH),
        out_shape=jax.ShapeDtypeStruct((bsz * seq, C_WIDTH), BF16),
        scratch_shapes=[pltpu.VMEM((ts, w.shape[1]), F32),
                        pltpu.VMEM((C_HEADS, C_VAL_DIM, C_KEY_DIM), F32)],
        compiler_params=_params(("parallel", "arbitrary")),
        name="hgrn2",
    )(x2, g, w, lbp, gout)


def kernel(x, rel_bias, hgrn_lb, mix_norm, ffn_norm, final_norm, ab_w_in, ab_idx_k_norm, ab_gmlp_norm, ab_w_s, ab_b_s, ab_w_out, c_w_in, c_out_norm, c_w_out, ffn_w_up, ffn_conv_w, ffn_conv_b, ffn_w_down):
    bsz, seq, d = x.shape
    depth = mix_norm.shape[0]
    x2 = x.reshape(bsz * seq, d)
    row = lambda a: a.reshape(1, -1)
    bias = _bias_tiles(rel_bias)
    wup_all, wdn_all = ffn_w_up.astype(BF16), ffn_w_down.astype(BF16)
    cb_all = ffn_conv_b.reshape(depth, 1, -1)
    for l in range(depth):
        if l % 2 == 0:
            e = l // 2
            w = ab_w_in[e]
            offs = [0]
            for s in (A_WIDTH, A_WIDTH, A_WIDTH, IDX_HEADS * IDX_DIM, IDX_DIM, IDX_HEADS,
                      B_WIDTH, B_WIDTH):
                offs.append(offs[-1] + s)
            seg = [w[:, offs[n]:offs[n + 1]] for n in range(8)]
            assert offs[4] == _C_END
            w_head = w[:, :_C_END].astype(BF16)
            w_tail = jnp.concatenate(
                [seg[6], seg[7], seg[4], seg[4], seg[5],
                 jnp.zeros((d, LANES - IDX_HEADS), w.dtype)], axis=1).astype(BF16)
            assert w_tail.shape[1] == _T_END
            gik = jnp.concatenate([ab_idx_k_norm[e], ab_idx_k_norm[e]])
            q, k, vt, iq, ik2, iwt, y_b = _proj_ab(
                x2, row(mix_norm[l]), w_head, w_tail, row(gik), row(ab_gmlp_norm[e]),
                ab_w_s[e], jnp.transpose(ab_b_s[e]), seq)
            y_a = _dsa(q, k, vt, iq, ik2, iwt, bias, bsz, seq)
            ys, wo = [y_a, y_b], ab_w_out[e]
        else:
            o_i = l // 2
            og = _hgrn(x2, row(mix_norm[l]), c_w_in[o_i].astype(BF16), hgrn_lb,
                       row(c_out_norm[o_i]), bsz, seq, l)
            ys, wo = [og], c_w_out[o_i]
        x2 = _ffn(x2, ys, wo.astype(BF16), row(ffn_norm[l]), wup_all, ffn_conv_w, cb_all,
                  wdn_all, row(final_norm), seq, l, l == depth - 1)
    return x2.reshape(bsz, seq, d)
```

```python
import functools
import math

import jax
import jax.numpy as jnp
from jax import lax
from jax.experimental import pallas as pl
from jax.experimental.pallas import tpu as pltpu

F32 = jnp.float32
BF16 = jnp.bfloat16
I32 = jnp.int32

D_MODEL = 1024
CHUNK = 64
A_HEADS = 8
A_HEAD_DIM = 64
A_WIDTH = A_HEADS * A_HEAD_DIM
IDX_HEADS = 16
IDX_DIM = 64
TOPK_MAX = 256
REL_BUCKETS = 32
REL_MAX_DIST = 128
B_GROUPS = 8
B_GROUP_DIM = 64
B_WIDTH = B_GROUPS * B_GROUP_DIM
B_CHUNK = 128
C_HEADS = 8
C_KEY_DIM = 128
C_VAL_DIM = 128
C_WIDTH = C_HEADS * C_VAL_DIM
D_FF = 2816
EPS = 1e-6

LANES = 128
VMEM_LIMIT = 56 * 1024 * 1024

ROW_TILE = 512
FFN_TILE = 512
DSA_TILE = 256
HG_TILE = 512
HG_SUB = 16
FF_COLS = 256
NEG = -1e30
M_INIT = -1e29
INT_MIN = -2 ** 31
SEARCH_HEAD_BITS = 24
SUM_ROWS = 16

_C_Q, _C_K, _C_V, _C_IQ, _C_END = 0, 512, 1024, 1536, 2560
_T_U, _T_VG, _T_IK, _T_IW, _T_END = 0, 512, 1024, 1152, 1280


def _params(sem):
    return pltpu.CompilerParams(dimension_semantics=sem, vmem_limit_bytes=VMEM_LIMIT)


def _resident(shape):
    nd = len(shape)
    return pl.BlockSpec(shape, lambda *_: (0,) * nd, pipeline_mode=pl.Buffered(1))


def _rms(x, g):
    ms = jnp.mean(x * x, axis=-1, keepdims=True)
    return x * lax.rsqrt(ms + EPS) * g


def _dot(a, b):
    return jnp.dot(a, b, preferred_element_type=F32)


def _dot_nt(a, b):
    return lax.dot_general(a, b, (((1,), (1,)), ((), ())), preferred_element_type=F32)


def _dot_tn(a, b):
    return lax.dot_general(a, b, (((0,), (0,)), ((), ())), preferred_element_type=F32)


def _t5_bucket(rel):
    half = REL_BUCKETS // 2
    max_exact = half // 2
    ret = jnp.where(rel > 0, half, 0)
    n = jnp.abs(rel)
    nf = jnp.maximum(n, max_exact).astype(F32)
    large = max_exact + (jnp.log(nf / max_exact) / math.log(REL_MAX_DIST / max_exact)
                         * (half - max_exact)).astype(I32)
    large = jnp.minimum(large, half - 1)
    return ret + jnp.where(n < max_exact, n, large)


def _bias_kernel(row_ref, o_ref):
    t = DSA_TILE
    x = jnp.broadcast_to(row_ref[0, 0], (t, 2 * t))
    o_ref[0, 0] = pltpu.roll(x, 0, 1, stride=1, stride_axis=0)[:, :t]


def _bias_tiles(rel_bias):
    t = DSA_TILE
    assert t >= REL_MAX_DIST
    m = jnp.arange(2 * t, dtype=I32)
    qk = jnp.where(m < t, m, m - 2 * t)
    rel = jnp.stack([-qk, -qk - t])
    table = rel_bias[_t5_bucket(rel)] - rel_bias[_t5_bucket(jnp.int32(-2 * t))]
    rows = jnp.transpose(table, (0, 2, 1)).reshape(2, A_HEADS, 1, 2 * t)
    return pl.pallas_call(
        _bias_kernel,
        grid=(2, A_HEADS),
        in_specs=[pl.BlockSpec((1, 1, 1, 2 * t), lambda d, h: (d, h, 0, 0))],
        out_specs=pl.BlockSpec((1, 1, t, t), lambda d, h: (d, h, 0, 0)),
        out_shape=jax.ShapeDtypeStruct((2, A_HEADS, t, t), F32),
        compiler_params=_params(("arbitrary", "arbitrary")),
        name="bias_tiles",
    )(rows)


def _proj_ab_kernel(x_ref, g_ref, w_ref, wt_ref, gik_ref, ggm_ref, ws_ref, bs_ref,
                    q_ref, k_ref, vt_ref, iq_ref, ik_ref, iwt_ref, yb_ref):
    tm = x_ref.shape[0]
    ri = lax.broadcasted_iota(I32, (B_CHUNK, B_CHUNK), 0) // CHUNK
    ci = lax.broadcasted_iota(I32, (B_CHUNK, B_CHUNK), 1) // CHUNK
    low = ri >= ci
    first = lax.broadcasted_iota(I32, (B_CHUNK, LANES), 1) < B_GROUP_DIM
    w_sp = [jnp.where(low, ws_ref[g], 0.0).astype(BF16) for g in range(B_GROUPS)]

    for rows in (slice(0, tm // 2), slice(tm // 2, tm)):
        h = _rms(x_ref[rows], g_ref[...]).astype(BF16)

        def proj(c0, c1, h=h):
            return _dot(h, w_ref[:, c0:c1])

        def tail(c0, c1, h=h):
            return _dot(h, wt_ref[:, c0:c1])

        u = jax.nn.gelu(tail(_T_U, _T_VG))
        vn = _rms(jax.nn.gelu(tail(_T_VG, _T_IK)), ggm_ref[...]).astype(BF16)
        q_ref[rows] = (proj(_C_Q, _C_K) * (A_HEAD_DIM ** -0.5)).astype(BF16)
        k_ref[rows] = proj(_C_K, _C_V).astype(BF16)
        vt_ref[:, rows] = jnp.transpose(proj(_C_V, _C_IQ)).astype(BF16)
        half = (_C_END - _C_IQ) // 2
        for j in range(2):
            iq_ref[rows, j * half:(j + 1) * half] = (
                proj(_C_IQ + j * half, _C_IQ + (j + 1) * half) * (IDX_DIM ** -0.5)).astype(BF16)
        ik_ref[rows] = _rms(tail(_T_IK, _T_IW), gik_ref[...]).astype(BF16)
        iwt_ref[:, rows] = jnp.transpose(tail(_T_IW, _T_END))[:IDX_HEADS] * (IDX_HEADS ** -0.5)
        for p in range(B_GROUPS // 2):
            b_e = bs_ref[:, 2 * p:2 * p + 1]
            b_o = bs_ref[:, 2 * p + 1:2 * p + 2]
            cols = slice(p * LANES, (p + 1) * LANES)
            for r in range(tm // 2 // B_CHUNK):
                sub = slice(r * B_CHUNK, (r + 1) * B_CHUNK)
                vp = vn[sub, cols]
                s = jnp.where(first, _dot(w_sp[2 * p], vp) + b_e, _dot(w_sp[2 * p + 1], vp) + b_o)
                yb_ref[rows.start + r * B_CHUNK:rows.start + (r + 1) * B_CHUNK, cols] = (
                    u[sub, cols] * s).astype(BF16)


def _proj_ab(x2, g, w, wt, gik, ggm, ws, bs_t, seq):
    m = x2.shape[0]
    tm = ROW_TILE
    tps = seq // tm
    row = lambda n: pl.BlockSpec((tm, n), lambda i: (i, 0))
    sds = jax.ShapeDtypeStruct
    consts = (g, w, wt, gik, ggm, ws, bs_t)
    return pl.pallas_call(
        _proj_ab_kernel,
        grid=(m // tm,),
        in_specs=[row(D_MODEL)] + [_resident(c.shape) for c in consts],
        out_specs=[row(A_WIDTH), row(A_WIDTH),
                   pl.BlockSpec((A_WIDTH, tm), lambda i: (i // tps, i % tps)),
                   row(IDX_HEADS * IDX_DIM), row(2 * IDX_DIM),
                   pl.BlockSpec((IDX_HEADS, tm), lambda i: (0, i)), row(B_WIDTH)],
        out_shape=[sds((m, A_WIDTH), BF16), sds((m, A_WIDTH), BF16),
                   sds((m // seq * A_WIDTH, seq), BF16), sds((m, IDX_HEADS * IDX_DIM), BF16),
                   sds((m, 2 * IDX_DIM), BF16), sds((IDX_HEADS, m), F32),
                   sds((m, B_WIDTH), BF16)],
        compiler_params=_params(("parallel",)),
        name="proj_ab",
    )(x2, *consts)


def _dsa_kernel(q_ref, k_ref, vt_ref, iq0_ref, ik0_ref, iwt0_ref, iqn_ref, ikn_ref, iwtn_ref,
                bias_ref, o_ref,
                keys_scr, madd_scr, iqm_scr, qm_scr, ot_scr, m_scr, alpha_scr, acc_scr, s_scr,
                *, top_k, nq, nsteps):
    t = DSA_TILE
    sub = 8
    step = pl.program_id(0)
    i = step % nq
    i1 = jnp.where(step + 1 < nsteps, (step + 1) % nq, 0)
    cur = step % 2
    nxt = 1 - cur
    first = lax.broadcasted_iota(I32, (t, LANES), 1) < A_HEAD_DIM
    allowed = (lax.broadcasted_iota(I32, (t, t), 0) // CHUNK
               <= lax.broadcasted_iota(I32, (t, t), 1) // CHUNK)

    def blk(kb):
        return pl.ds(pl.multiple_of(kb * t, t), t)

    def split_heads(src_ref, dst_scr, heads):
        for h in range(heads):
            pair = src_ref[:, (h // 2) * LANES:(h // 2 + 1) * LANES]
            dst_scr[h] = jnp.where(first if h % 2 == 0 else ~first, pair, jnp.zeros_like(pair))

    def score_keys(ik_ref, iwt_ref, kb, slot, diag):
        ikb = ik_ref[blk(kb), :]
        acc = jnp.zeros((t, t), F32)
        for h in range(IDX_HEADS):
            s = _dot_nt(ikb, iqm_scr[h])
            acc = acc + jnp.maximum(s, 0.0) * iwt_ref[h:h + 1, :]
        if diag:
            acc = jnp.where(allowed, acc, -jnp.inf)
        keys_scr[slot, blk(kb), :] = acc

    @pl.when(step == 0)
    def _():
        split_heads(iq0_ref, iqm_scr, IDX_HEADS)
        score_keys(ik0_ref, iwt0_ref, 0, 0, True)

    split_heads(iqn_ref, iqm_scr, IDX_HEADS)
    split_heads(q_ref, qm_scr, A_HEADS)

    @pl.when(i == 0)
    def _():
        madd_scr[0:t, :] = jnp.where(allowed, 0.0, NEG)

    @pl.when(i > 0)
    def _():
        def count(cmp, thr):
            nacc = 4

            def body(kb, cnts):
                kk = keys_scr[cur, blk(kb), :]
                cnts = list(cnts)
                for r in range(t // sub):
                    cnts[r % nacc] = cnts[r % nacc] + jnp.where(
                        cmp(kk[r * sub:(r + 1) * sub], thr), 1.0, 0.0)
                return tuple(cnts)
            cnts = lax.fori_loop(0, i + 1, body, (jnp.zeros((sub, t), F32),) * nacc)
            return jnp.sum(sum(cnts), axis=0, keepdims=True)

        ge = lambda a, b: a >= b
        gt = lambda a, b: a > b

        def as_score(key):
            bits = key ^ ((key >> 31) & jnp.int32(0x7FFFFFFF))
            return lax.bitcast_convert_type(bits, F32)

        def bit_body(it, c):
            key, n_ge = c
            cand = key + (jnp.int32(1) << (31 - it))
            cnt = count(ge, as_score(cand))
            ok = cnt >= top_k
            return jnp.where(ok, cand, key), jnp.where(ok, cnt, n_ge)

        def unsettled(n_ge):
            return jnp.max(jnp.abs(n_ge - top_k))

        def tail_body(c):
            it, _, key, n_ge = c
            key, n_ge = bit_body(it, (key, n_ge))
            return it + 1, unsettled(n_ge), key, n_ge

        key, n_ge = lax.fori_loop(
            0, SEARCH_HEAD_BITS, bit_body,
            (jnp.full((sub, t), INT_MIN, I32), jnp.full((1, t), jnp.inf, F32)))
        _, _, key, n_ge = lax.while_loop(
            lambda c: (c[0] < 32) & (c[1] > 0), tail_body,
            (jnp.int32(SEARCH_HEAD_BITS), unsettled(n_ge), key, n_ge))
        thr = as_score(key)
        thr_row = thr[0:1]

        def plain_body(kb, c):
            madd_scr[blk(kb), :] = jnp.where(keys_scr[cur, blk(kb), :] >= thr_row, 0.0, NEG)
            return c

        lax.fori_loop(0, i + 1, plain_body, 0)

        @pl.when(jnp.max(n_ge) > top_k)
        def _():
            need = top_k - count(gt, thr)
            lower = (lax.broadcasted_iota(I32, (t, t), 1)
                     < lax.broadcasted_iota(I32, (t, t), 0))
            lower = jnp.where(lower, 1.0, 0.0).astype(BF16)

            def tie_body(kb, run):
                kk = keys_scr[cur, blk(kb), :]
                eq = jnp.where(kk == thr_row, 1.0, 0.0)
                rank = run + _dot(lower, eq.astype(BF16))
                sel = (kk > thr_row) | ((kk == thr_row) & (rank < need))
                madd_scr[blk(kb), :] = jnp.where(sel, 0.0, NEG)
                return run + jnp.sum(eq, axis=0, keepdims=True)

            lax.fori_loop(0, i + 1, tie_body, jnp.zeros((1, t), F32))

    ones = jnp.ones((SUM_ROWS, t), BF16)

    m_scr[...] = jnp.full(m_scr.shape, M_INIT, F32)
    acc_scr[...] = jnp.zeros(acc_scr.shape, F32)

    def att_block(kb, didx, score_next):
        madd = madd_scr[blk(kb), :]
        for h in range(A_HEADS):
            cols = slice((h // 2) * LANES, (h // 2 + 1) * LANES)
            s = _dot_nt(k_ref[blk(kb), cols], qm_scr[h]) + madd
            if didx is not None:
                s = s + bias_ref[didx, h]
            s_scr[h] = s
            m = m_scr[h]
            m_new = jnp.maximum(m, jnp.max(s, axis=0, keepdims=True))
            alpha_scr[h] = jnp.exp(m - m_new)
            m_scr[h] = m_new

        if score_next:
            score_keys(ikn_ref, iwtn_ref, kb, nxt, False)

        for h in range(A_HEADS):
            pr = jnp.exp(s_scr[h] - m_scr[h]).astype(BF16)
            v1 = jnp.concatenate(
                [vt_ref[h * A_HEAD_DIM:(h + 1) * A_HEAD_DIM, blk(kb)], ones], axis=0)
            acc_scr[h] = alpha_scr[h] * acc_scr[h] + _dot(v1, pr)

    def att_body(kb, c, didx, score_next):
        att_block(kb, didx, score_next)
        return c

    same_seq = i1 > 0
    n_far = jnp.maximum(i - 1, 0)
    for score_next in (True, False):
        on = same_seq if score_next else jnp.logical_not(same_seq)
        lax.fori_loop(0, jnp.where(on, n_far, 0),
                      functools.partial(att_body, didx=None, score_next=score_next), 0)
    for score_next in (True, False):
        on = same_seq if score_next else jnp.logical_not(same_seq)
        lax.fori_loop(n_far, jnp.where(on, i, n_far),
                      functools.partial(att_body, didx=1, score_next=score_next), 0)
    for score_next in (True, False):
        on = same_seq if score_next else jnp.logical_not(same_seq)
        lax.fori_loop(0, jnp.where(on, 1, 0),
                      lambda _, c, sn=score_next: att_body(i, c, 0, sn), 0)
    score_keys(ikn_ref, iwtn_ref, i1, nxt, True)
    for h in range(A_HEADS):
        acc = acc_scr[h]
        ot_scr[h * A_HEAD_DIM:(h + 1) * A_HEAD_DIM, :] = (
            acc[:A_HEAD_DIM] * (1.0 / acc[A_HEAD_DIM:A_HEAD_DIM + 1]))
    o_ref[...] = jnp.transpose(ot_scr[...]).astype(BF16)


def _dsa(q, k, vt, iq, ik2, iwt, bias, bsz, seq):
    t = DSA_TILE
    nq = seq // t
    top_k = min(TOPK_MAX, seq // 4)
    assert t <= top_k and t % CHUNK == 0 and seq % t == 0
    nsteps = bsz * nq
    nxt = lambda s: jnp.minimum(s + 1, nsteps - 1)
    iq_w, ik_w = IDX_HEADS * IDX_DIM, 2 * IDX_DIM
    return pl.pallas_call(
        functools.partial(_dsa_kernel, top_k=top_k, nq=nq, nsteps=nsteps),
        grid=(nsteps,),
        in_specs=[pl.BlockSpec((t, A_WIDTH), lambda s: (s, 0)),
                  pl.BlockSpec((seq, A_WIDTH), lambda s: (s // nq, 0)),
                  pl.BlockSpec((A_WIDTH, seq), lambda s: (s // nq, 0)),
                  pl.BlockSpec((t, iq_w), lambda s: (0, 0)),
                  pl.BlockSpec((seq, ik_w), lambda s: (0, 0)),
                  pl.BlockSpec((IDX_HEADS, t), lambda s: (0, 0)),
                  pl.BlockSpec((t, iq_w), lambda s: (nxt(s), 0)),
                  pl.BlockSpec((seq, ik_w), lambda s: (nxt(s) // nq, 0)),
                  pl.BlockSpec((IDX_HEADS, t), lambda s: (0, nxt(s))),
                  _resident(bias.shape)],
        out_specs=pl.BlockSpec((t, A_WIDTH), lambda s: (s, 0)),
        out_shape=jax.ShapeDtypeStruct((bsz * seq, A_WIDTH), BF16),
        scratch_shapes=[pltpu.VMEM((2, seq, t), F32), pltpu.VMEM((seq, t), F32),
                        pltpu.VMEM((IDX_HEADS, t, LANES), BF16),
                        pltpu.VMEM((A_HEADS, t, LANES), BF16),
                        pltpu.VMEM((A_WIDTH, t), F32),
                        pltpu.VMEM((A_HEADS, 1, t), F32),
                        pltpu.VMEM((A_HEADS, 1, t), F32),
                        pltpu.VMEM((A_HEADS, A_HEAD_DIM + SUM_ROWS, t), F32),
                        pltpu.VMEM((A_HEADS, t, t), F32)],
        compiler_params=_params(("arbitrary",)),
        name="dsa",
    )(q, k, vt, iq, ik2, iwt, iq, ik2, iwt, bias)


HALO = 16


def _ffn_kernel(*refs, n_y, tiles_per_seq, final_norm):
    x_ref, xh_ref = refs[0], refs[1]
    y_refs = refs[2:2 + 2 * n_y]
    (wo_ref, g_ref, wup_ref, cw_ref, cb_ref, wdn_ref, gfin_ref, o_ref,
     ua_scr, ub_scr, act_scr, acc_scr) = refs[2 + 2 * n_y:]
    tm = x_ref.shape[0]

    half = tm // 2
    halves = (slice(0, half), slice(half, tm))

    def mixed(x_r, ys, rows):
        acc = x_r[rows]
        r0 = 0
        for y_r in ys:
            n = y_r.shape[1]
            acc = acc + _dot(y_r[rows], wo_ref[r0:r0 + n, :])
            r0 += n
        return acc

    xh = mixed(xh_ref, y_refs[1::2], slice(0, HALO))
    xs = [mixed(x_ref, y_refs[0::2], r) for r in halves]
    keep = jnp.where(pl.program_id(0) % tiles_per_seq == 0, 0.0, 1.0)
    hh = (_rms(xh, g_ref[...]) * keep).astype(BF16)

    def conv(scr, c0, rows):
        w = cw_ref[:, c0:c0 + FF_COLS]
        r0, r1 = HALO + rows.start, HALO + rows.stop
        return (cb_ref[:, c0:c0 + FF_COLS]
                + w[2:3] * scr[r0:r1]
                + w[1:2] * scr[r0 - 1:r1 - 1]
                + w[0:1] * scr[r0 - 2:r1 - 2])

    def gate(j, rows):
        ca, cb = j * FF_COLS, D_FF + j * FF_COLS
        act_scr[j % 2, rows] = (jax.nn.silu(conv(ua_scr.at[j % 2], ca, rows))
                                * conv(ub_scr.at[j % 2], cb, rows)).astype(BF16)

    nj = D_FF // FF_COLS

    def up(j, hs, rows=slice(None)):
        ua_scr[j % 2, rows] = _dot(hs, wup_ref[:, j * FF_COLS:(j + 1) * FF_COLS])
        ub_scr[j % 2, rows] = _dot(hs, wup_ref[:, D_FF + j * FF_COLS:D_FF + (j + 1) * FF_COLS])

    def down(j, rows=slice(None)):
        return _dot(act_scr[j % 2, rows], wdn_ref[j * FF_COLS:(j + 1) * FF_COLS, :])

    hn0 = _rms(xs[0], g_ref[...]).astype(BF16)
    up(0, jnp.concatenate([hh, hn0], axis=0), slice(0, half + HALO))
    hn1 = _rms(xs[1], g_ref[...]).astype(BF16)
    up(0, jnp.concatenate([hn0[half - HALO:], hn1], axis=0), slice(half, tm + HALO))
    he = jnp.concatenate([hh, hn0, hn1], axis=0)

    for j in range(nj):
        if j + 1 < nj:
            up(j + 1, he)
        if j == 1:
            acc_scr[...] = down(0)
        elif j > 1:
            acc_scr[...] += down(j - 1)
        if j + 1 < nj:
            gate(j, slice(0, tm))
    for r, x in zip(halves, xs):
        gate(nj - 1, r)
        acc_scr[r] += down(nj - 1, r)
        y = x + acc_scr[r]
        if final_norm:
            y = _rms(y, gfin_ref[...])
        o_ref[r] = y


def _layer_resident(a, layer):
    nd = a.ndim - 1
    return pl.BlockSpec((None,) + a.shape[1:], lambda *_: (layer,) + (0,) * nd,
                        pipeline_mode=pl.Buffered(1))


def _ffn(x2, ys, wo, g, wup, cw, cb, wdn, gfin, seq, layer, final_norm):
    m = x2.shape[0]
    tm = FFN_TILE
    assert seq % tm == 0 and D_FF % FF_COLS == 0 and tm % HALO == 0
    row = lambda n: pl.BlockSpec((tm, n), lambda i: (i, 0))
    halo = lambda n: pl.BlockSpec((HALO, n), lambda i: (jnp.maximum(i * (tm // HALO) - 1, 0), 0))
    y_specs, y_args = [], []
    for y in ys:
        y_specs += [row(y.shape[1]), halo(y.shape[1])]
        y_args += [y, y]
    consts = (wo, g, wup, cw, cb, wdn, gfin)
    const_specs = [_layer_resident(c, layer) if c is wup or c is cw or c is cb or c is wdn
                   else _resident(c.shape) for c in consts]
    return pl.pallas_call(
        functools.partial(_ffn_kernel, n_y=len(ys), tiles_per_seq=seq // tm,
                          final_norm=final_norm),
        grid=(m // tm,),
        in_specs=[row(D_MODEL), halo(D_MODEL)] + y_specs + const_specs,
        out_specs=row(D_MODEL),
        out_shape=jax.ShapeDtypeStruct((m, D_MODEL), F32),
        scratch_shapes=[pltpu.VMEM((2, tm + HALO, FF_COLS), F32),
                        pltpu.VMEM((2, tm + HALO, FF_COLS), F32),
                        pltpu.VMEM((2, tm, FF_COLS), BF16),
                        pltpu.VMEM((tm, D_MODEL), F32)],
        compiler_params=_params(("parallel",)),
        name="conv_ffn",
    )(x2, x2, *y_args, *consts)


def _hgrn_kernel(x_ref, g_ref, w_ref, lbp_ref, gout_ref, o_ref, proj_scr, st_scr, *, layer):
    ts = x_ref.shape[0]
    kw = C_HEADS * C_KEY_DIM
    c = CHUNK
    nsub = c // HG_SUB

    @pl.when(pl.program_id(1) == 0)
    def _():
        st_scr[...] = jnp.zeros(st_scr.shape, F32)

    h = _rms(x_ref[...], g_ref[...]).astype(BF16)

    def project(c0, c1):
        pw = 512
        for n in range(c0, c1, pw):
            proj_scr[:, n:n + pw] = _dot(h, w_ref[:, n:n + pw])

    lbp = lbp_ref[...]
    e = jnp.exp(lbp - jnp.max(lbp, axis=0, keepdims=True))
    sm = e / jnp.sum(e, axis=0, keepdims=True)
    lb = jnp.sum(sm[1:layer + 1], axis=0, keepdims=True) if layer > 0 else jnp.zeros((1, kw), F32)

    ri = lax.broadcasted_iota(I32, (c, c), 0)
    ci = lax.broadcasted_iota(I32, (c, c), 1)
    r3 = lax.broadcasted_iota(I32, (c, 3 * c), 0)
    c3 = lax.broadcasted_iota(I32, (c, 3 * c), 1) % c
    tri3 = jnp.where(r3 >= c3, 1.0, 0.0).astype(BF16)
    dmask = (ri >= ci) & (ri // HG_SUB == ci // HG_SUB)

    def gates(rows):
        gg = lb + (1.0 - lb) * jax.nn.sigmoid(proj_scr[rows, kw:2 * kw])
        lg = jnp.log(gg)
        hi = lg.astype(BF16)
        r1 = lg - hi.astype(F32)
        mid = r1.astype(BF16)
        lo = (r1 - mid.astype(F32)).astype(BF16)
        return gg, jnp.concatenate([hi, mid, lo], axis=0)

    def decay_products(rows, gg, b_all):
        out = []
        for hd in range(C_HEADS):
            kc = slice(hd * C_KEY_DIM, (hd + 1) * C_KEY_DIM)
            q = jax.nn.silu(proj_scr[rows, hd * C_KEY_DIM:(hd + 1) * C_KEY_DIM])
            kk = 1.0 - gg[:, kc]
            b = b_all[:, kc]
            qs, ks = [], []
            for j in range(nsub - 1):
                e0, e1 = j * HG_SUB, (j + 1) * HG_SUB
                rj = b[e1 - 1:e1]
                qs.append(jnp.concatenate(
                    [jnp.zeros((e1, C_KEY_DIM), F32), q[e1:] * jnp.exp(b[e1:] - rj)], axis=0))
                kj = kk[e0:e1] * jnp.exp(rj - b[e0:e1])
                parts = [kj, jnp.zeros((c - e1, C_KEY_DIM), F32)]
                if e0:
                    parts = [jnp.zeros((e0, C_KEY_DIM), F32)] + parts
                ks.append(jnp.concatenate(parts, axis=0))
            a_off = _dot_nt(jnp.concatenate(qs, axis=1).astype(BF16),
                            jnp.concatenate(ks, axis=1).astype(BF16))
            bm = jnp.concatenate(
                [jnp.broadcast_to(b[j * HG_SUB + HG_SUB // 2:j * HG_SUB + HG_SUB // 2 + 1],
                                  (HG_SUB, C_KEY_DIM)) for j in range(nsub)], axis=0)
            a_dg = _dot_nt((q * jnp.exp(b - bm)).astype(BF16), (kk * jnp.exp(bm - b)).astype(BF16))
            a = (a_off + jnp.where(dmask, a_dg, 0.0)).astype(BF16)
            bl = b[c - 1:c]
            out.append([(q * jnp.exp(b)).astype(BF16), a,
                        (kk * jnp.exp(bl - b)).astype(BF16), jnp.exp(bl)])
        return out

    def values(rows, hd):
        return proj_scr[rows, 2 * kw + hd * C_VAL_DIM:2 * kw + (hd + 1) * C_VAL_DIM].astype(BF16)

    def back(rows, parts, intras):
        sts = [st_scr[hd] for hd in range(C_HEADS)]
        inters = [_dot_nt(p[0], st.astype(BF16)) for p, st in zip(parts, sts)]
        upds = [_dot_tn(values(rows, hd), p[2]) for hd, p in enumerate(parts)]
        for hd, p in enumerate(parts):
            st_scr[hd] = sts[hd] * p[3] + upds[hd]
            gate = proj_scr[rows, 2 * kw + C_WIDTH + hd * C_VAL_DIM:
                            2 * kw + C_WIDTH + (hd + 1) * C_VAL_DIM]
            o_ref[rows, hd * C_VAL_DIM:(hd + 1) * C_VAL_DIM] = (
                _rms(inters[hd] + intras[hd], gout_ref[...]) * jax.nn.silu(gate)).astype(BF16)

    rows = [slice(n * c, (n + 1) * c) for n in range(ts // c)]
    project(kw, 2 * kw)
    gate_parts = [gates(r) for r in rows]
    project(0, kw)
    cums = [_dot(tri3, lg3) for _, lg3 in gate_parts]
    project(2 * kw, 2 * kw + C_WIDTH)
    parts = [decay_products(r, gg, b_all) for r, (gg, _), b_all in zip(rows, gate_parts, cums)]
    project(2 * kw + C_WIDTH, 2 * kw + 2 * C_WIDTH)
    intras = [[_dot(p[1], values(r, hd)) for hd, p in enumerate(ps)]
              for r, ps in zip(rows, parts)]
    for r, ps, it in zip(rows, parts, intras):
        back(r, ps, it)


def _hgrn(x2, g, w, lbp, gout, bsz, seq, layer):
    ts = HG_TILE
    nt = seq // ts
    assert seq % ts == 0 and ts % CHUNK == 0 and CHUNK % HG_SUB == 0
    row = lambda n: pl.BlockSpec((ts, n), lambda b, i: (b * nt + i, 0))
    return pl.pallas_call(
        functools.partial(_hgrn_kernel, layer=layer),
        grid=(bsz, nt),
        in_specs=[row(D_MODEL), _resident(g.shape), _resident(w.shape), _resident(lbp.shape),
                  _resident(gout.shape)],
        out_specs=row(C_WIDTH),
        out_shape=jax.ShapeDtypeStruct((bsz * seq, C_WIDTH), BF16),
        scratch_shapes=[pltpu.VMEM((ts, w.shape[1]), F32),
                        pltpu.VMEM((C_HEADS, C_VAL_DIM, C_KEY_DIM), F32)],
        compiler_params=_params(("parallel", "arbitrary")),
        name="hgrn2",
    )(x2, g, w, lbp, gout)


def kernel(x, rel_bias, hgrn_lb, mix_norm, ffn_norm, final_norm, ab_w_in, ab_idx_k_norm, ab_gmlp_norm, ab_w_s, ab_b_s, ab_w_out, c_w_in, c_out_norm, c_w_out, ffn_w_up, ffn_conv_w, ffn_conv_b, ffn_w_down):
    bsz, seq, d = x.shape
    depth = mix_norm.shape[0]
    x2 = x.reshape(bsz * seq, d)
    row = lambda a: a.reshape(1, -1)
    bias = _bias_tiles(rel_bias)
    wup_all, wdn_all = ffn_w_up.astype(BF16), ffn_w_down.astype(BF16)
    cb_all = ffn_conv_b.reshape(depth, 1, -1)
    for l in range(depth):
        if l % 2 == 0:
            e = l // 2
            w = ab_w_in[e]
            offs = [0]
            for s in (A_WIDTH, A_WIDTH, A_WIDTH, IDX_HEADS * IDX_DIM, IDX_DIM, IDX_HEADS,
                      B_WIDTH, B_WIDTH):
                offs.append(offs[-1] + s)
            seg = [w[:, offs[n]:offs[n + 1]] for n in range(8)]
            assert offs[4] == _C_END
            w_head = w[:, :_C_END].astype(BF16)
            w_tail = jnp.concatenate(
                [seg[6], seg[7], seg[4], seg[4], seg[5],
                 jnp.zeros((d, LANES - IDX_HEADS), w.dtype)], axis=1).astype(BF16)
            assert w_tail.shape[1] == _T_END
            gik = jnp.concatenate([ab_idx_k_norm[e], ab_idx_k_norm[e]])
            q, k, vt, iq, ik2, iwt, y_b = _proj_ab(
                x2, row(mix_norm[l]), w_head, w_tail, row(gik), row(ab_gmlp_norm[e]),
                ab_w_s[e], jnp.transpose(ab_b_s[e]), seq)
            y_a = _dsa(q, k, vt, iq, ik2, iwt, bias, bsz, seq)
            ys, wo = [y_a, y_b], ab_w_out[e]
        else:
            o_i = l // 2
            og = _hgrn(x2, row(mix_norm[l]), c_w_in[o_i].astype(BF16), hgrn_lb,
                       row(c_out_norm[o_i]), bsz, seq, l)
            ys, wo = [og], c_w_out[o_i]
        x2 = _ffn(x2, ys, wo.astype(BF16), row(ffn_norm[l]), wup_all, ffn_conv_w, cb_all,
                  wdn_all, row(final_norm), seq, l, l == depth - 1)
    return x2.reshape(bsz, seq, d)
```

```python
import functools
import math

import jax
import jax.numpy as jnp
from jax import lax
from jax.experimental import pallas as pl
from jax.experimental.pallas import tpu as pltpu

F32 = jnp.float32
BF16 = jnp.bfloat16
I32 = jnp.int32

D_MODEL = 1024
CHUNK = 64
A_HEADS = 8
A_HEAD_DIM = 64
A_WIDTH = A_HEADS * A_HEAD_DIM
IDX_HEADS = 16
IDX_DIM = 64
TOPK_MAX = 256
REL_BUCKETS = 32
REL_MAX_DIST = 128
B_GROUPS = 8
B_GROUP_DIM = 64
B_WIDTH = B_GROUPS * B_GROUP_DIM
B_CHUNK = 128
C_HEADS = 8
C_KEY_DIM = 128
C_VAL_DIM = 128
C_WIDTH = C_HEADS * C_VAL_DIM
D_FF = 2816
EPS = 1e-6

LANES = 128
VMEM_LIMIT = 56 * 1024 * 1024

ROW_TILE = 512
FFN_TILE = 512
DSA_TILE = 256
HG_TILE = 512
HG_SUB = 16
FF_COLS = 256
FF_AHEAD = 10
NEG = -1e30
M_INIT = -1e29
INT_MIN = -2 ** 31
SEARCH_HEAD_BITS = 24
SUM_ROWS = 16

_C_Q, _C_K, _C_V, _C_IQ, _C_END = 0, 512, 1024, 1536, 2560
_T_U, _T_VG, _T_IK, _T_IW, _T_END = 0, 512, 1024, 1152, 1280


def _params(sem):
    return pltpu.CompilerParams(dimension_semantics=sem, vmem_limit_bytes=VMEM_LIMIT)


def _resident(shape):
    nd = len(shape)
    return pl.BlockSpec(shape, lambda *_: (0,) * nd, pipeline_mode=pl.Buffered(1))


def _rms(x, g):
    ms = jnp.mean(x * x, axis=-1, keepdims=True)
    return x * lax.rsqrt(ms + EPS) * g


def _dot(a, b):
    return jnp.dot(a, b, preferred_element_type=F32)


def _dot_nt(a, b):
    return lax.dot_general(a, b, (((1,), (1,)), ((), ())), preferred_element_type=F32)


def _dot_tn(a, b):
    return lax.dot_general(a, b, (((0,), (0,)), ((), ())), preferred_element_type=F32)


def _t5_bucket(rel):
    half = REL_BUCKETS // 2
    max_exact = half // 2
    ret = jnp.where(rel > 0, half, 0)
    n = jnp.abs(rel)
    nf = jnp.maximum(n, max_exact).astype(F32)
    large = max_exact + (jnp.log(nf / max_exact) / math.log(REL_MAX_DIST / max_exact)
                         * (half - max_exact)).astype(I32)
    large = jnp.minimum(large, half - 1)
    return ret + jnp.where(n < max_exact, n, large)


def _bias_kernel(row_ref, o_ref):
    t = DSA_TILE
    x = jnp.broadcast_to(row_ref[0, 0], (t, 2 * t))
    o_ref[0, 0] = pltpu.roll(x, 0, 1, stride=1, stride_axis=0)[:, :t]


def _bias_tiles(rel_bias):
    t = DSA_TILE
    assert t >= REL_MAX_DIST
    m = jnp.arange(2 * t, dtype=I32)
    qk = jnp.where(m < t, m, m - 2 * t)
    rel = jnp.stack([-qk, -qk - t])
    table = rel_bias[_t5_bucket(rel)] - rel_bias[_t5_bucket(jnp.int32(-2 * t))]
    rows = jnp.transpose(table, (0, 2, 1)).reshape(2, A_HEADS, 1, 2 * t)
    return pl.pallas_call(
        _bias_kernel,
        grid=(2, A_HEADS),
        in_specs=[pl.BlockSpec((1, 1, 1, 2 * t), lambda d, h: (d, h, 0, 0))],
        out_specs=pl.BlockSpec((1, 1, t, t), lambda d, h: (d, h, 0, 0)),
        out_shape=jax.ShapeDtypeStruct((2, A_HEADS, t, t), F32),
        compiler_params=_params(("arbitrary", "arbitrary")),
        name="bias_tiles",
    )(rows)


def _proj_ab_kernel(x_ref, g_ref, w_ref, wt_ref, gik_ref, ggm_ref, ws_ref, bs_ref,
                    q_ref, k_ref, vt_ref, iq_ref, ik_ref, iwt_ref, yb_ref):
    tm = x_ref.shape[0]
    ri = lax.broadcasted_iota(I32, (B_CHUNK, B_CHUNK), 0) // CHUNK
    ci = lax.broadcasted_iota(I32, (B_CHUNK, B_CHUNK), 1) // CHUNK
    low = ri >= ci
    first = lax.broadcasted_iota(I32, (B_CHUNK, LANES), 1) < B_GROUP_DIM
    w_sp = [jnp.where(low, ws_ref[g], 0.0).astype(BF16) for g in range(B_GROUPS)]

    for rows in (slice(0, tm // 2), slice(tm // 2, tm)):
        h = _rms(x_ref[rows], g_ref[...]).astype(BF16)

        def proj(c0, c1, h=h):
            return _dot(h, w_ref[:, c0:c1])

        def tail(c0, c1, h=h):
            return _dot(h, wt_ref[:, c0:c1])

        u = jax.nn.gelu(tail(_T_U, _T_VG))
        vn = _rms(jax.nn.gelu(tail(_T_VG, _T_IK)), ggm_ref[...]).astype(BF16)
        q_ref[rows] = (proj(_C_Q, _C_K) * (A_HEAD_DIM ** -0.5)).astype(BF16)
        k_ref[rows] = proj(_C_K, _C_V).astype(BF16)
        vt_ref[:, rows] = jnp.transpose(proj(_C_V, _C_IQ)).astype(BF16)
        half = (_C_END - _C_IQ) // 2
        for j in range(2):
            iq_ref[rows, j * half:(j + 1) * half] = (
                proj(_C_IQ + j * half, _C_IQ + (j + 1) * half) * (IDX_DIM ** -0.5)).astype(BF16)
        ik_ref[rows] = _rms(tail(_T_IK, _T_IW), gik_ref[...]).astype(BF16)
        iwt_ref[:, rows] = jnp.transpose(tail(_T_IW, _T_END))[:IDX_HEADS] * (IDX_HEADS ** -0.5)
        for p in range(B_GROUPS // 2):
            b_e = bs_ref[:, 2 * p:2 * p + 1]
            b_o = bs_ref[:, 2 * p + 1:2 * p + 2]
            cols = slice(p * LANES, (p + 1) * LANES)
            for r in range(tm // 2 // B_CHUNK):
                sub = slice(r * B_CHUNK, (r + 1) * B_CHUNK)
                vp = vn[sub, cols]
                s = jnp.where(first, _dot(w_sp[2 * p], vp) + b_e, _dot(w_sp[2 * p + 1], vp) + b_o)
                yb_ref[rows.start + r * B_CHUNK:rows.start + (r + 1) * B_CHUNK, cols] = (
                    u[sub, cols] * s).astype(BF16)


def _proj_ab(x2, g, w, wt, gik, ggm, ws, bs_t, seq):
    m = x2.shape[0]
    tm = ROW_TILE
    tps = seq // tm
    row = lambda n: pl.BlockSpec((tm, n), lambda i: (i, 0))
    sds = jax.ShapeDtypeStruct
    consts = (g, w, wt, gik, ggm, ws, bs_t)
    return pl.pallas_call(
        _proj_ab_kernel,
        grid=(m // tm,),
        in_specs=[row(D_MODEL)] + [_resident(c.shape) for c in consts],
        out_specs=[row(A_WIDTH), row(A_WIDTH),
                   pl.BlockSpec((A_WIDTH, tm), lambda i: (i // tps, i % tps)),
                   row(IDX_HEADS * IDX_DIM), row(2 * IDX_DIM),
                   pl.BlockSpec((IDX_HEADS, tm), lambda i: (0, i)), row(B_WIDTH)],
        out_shape=[sds((m, A_WIDTH), BF16), sds((m, A_WIDTH), BF16),
                   sds((m // seq * A_WIDTH, seq), BF16), sds((m, IDX_HEADS * IDX_DIM), BF16),
                   sds((m, 2 * IDX_DIM), BF16), sds((IDX_HEADS, m), F32),
                   sds((m, B_WIDTH), BF16)],
        compiler_params=_params(("parallel",)),
        name="proj_ab",
    )(x2, *consts)


def _dsa_kernel(q_ref, k_ref, vt_ref, iq0_ref, ik0_ref, iwt0_ref, iqn_ref, ikn_ref, iwtn_ref,
                bias_ref, o_ref,
                keys_scr, madd_scr, iqm_scr, qm_scr, ot_scr, m_scr, alpha_scr, acc_scr, s_scr,
                *, top_k, nq, nsteps):
    t = DSA_TILE
    sub = 8
    step = pl.program_id(0)
    i = step % nq
    i1 = jnp.where(step + 1 < nsteps, (step + 1) % nq, 0)
    cur = step % 2
    nxt = 1 - cur
    first = lax.broadcasted_iota(I32, (t, LANES), 1) < A_HEAD_DIM
    allowed = (lax.broadcasted_iota(I32, (t, t), 0) // CHUNK
               <= lax.broadcasted_iota(I32, (t, t), 1) // CHUNK)

    def blk(kb):
        return pl.ds(pl.multiple_of(kb * t, t), t)

    def split_heads(src_ref, dst_scr, heads):
        for h in range(heads):
            pair = src_ref[:, (h // 2) * LANES:(h // 2 + 1) * LANES]
            dst_scr[h] = jnp.where(first if h % 2 == 0 else ~first, pair, jnp.zeros_like(pair))

    def score_keys(ik_ref, iwt_ref, kb, slot, diag):
        ikb = ik_ref[blk(kb), :]
        acc = jnp.zeros((t, t), F32)
        for h in range(IDX_HEADS):
            s = _dot_nt(ikb, iqm_scr[h])
            acc = acc + jnp.maximum(s, 0.0) * iwt_ref[h:h + 1, :]
        if diag:
            acc = jnp.where(allowed, acc, -jnp.inf)
        keys_scr[slot, blk(kb), :] = acc

    @pl.when(step == 0)
    def _():
        split_heads(iq0_ref, iqm_scr, IDX_HEADS)
        score_keys(ik0_ref, iwt0_ref, 0, 0, True)

    split_heads(iqn_ref, iqm_scr, IDX_HEADS)
    split_heads(q_ref, qm_scr, A_HEADS)

    @pl.when(i == 0)
    def _():
        madd_scr[0:t, :] = jnp.where(allowed, 0.0, NEG)

    @pl.when(i > 0)
    def _():
        def count(cmp, thr):
            nacc = 4

            def body(kb, cnts):
                kk = keys_scr[cur, blk(kb), :]
                cnts = list(cnts)
                for r in range(t // sub):
                    cnts[r % nacc] = cnts[r % nacc] + jnp.where(
                        cmp(kk[r * sub:(r + 1) * sub], thr), 1.0, 0.0)
                return tuple(cnts)
            cnts = lax.fori_loop(0, i + 1, body, (jnp.zeros((sub, t), F32),) * nacc)
            return jnp.sum(sum(cnts), axis=0, keepdims=True)

        ge = lambda a, b: a >= b
        gt = lambda a, b: a > b

        def as_score(key):
            bits = key ^ ((key >> 31) & jnp.int32(0x7FFFFFFF))
            return lax.bitcast_convert_type(bits, F32)

        def bit_body(it, c):
            key, n_ge = c
            cand = key + (jnp.int32(1) << (31 - it))
            cnt = count(ge, as_score(cand))
            ok = cnt >= top_k
            return jnp.where(ok, cand, key), jnp.where(ok, cnt, n_ge)

        def unsettled(n_ge):
            return jnp.max(jnp.abs(n_ge - top_k))

        def tail_body(c):
            it, _, key, n_ge = c
            key, n_ge = bit_body(it, (key, n_ge))
            return it + 1, unsettled(n_ge), key, n_ge

        key, n_ge = lax.fori_loop(
            0, SEARCH_HEAD_BITS, bit_body,
            (jnp.full((sub, t), INT_MIN, I32), jnp.full((1, t), jnp.inf, F32)))
        _, _, key, n_ge = lax.while_loop(
            lambda c: (c[0] < 32) & (c[1] > 0), tail_body,
            (jnp.int32(SEARCH_HEAD_BITS), unsettled(n_ge), key, n_ge))
        thr = as_score(key)
        thr_row = thr[0:1]

        def plain_body(kb, c):
            madd_scr[blk(kb), :] = jnp.where(keys_scr[cur, blk(kb), :] >= thr_row, 0.0, NEG)
            return c

        lax.fori_loop(0, i + 1, plain_body, 0)

        @pl.when(jnp.max(n_ge) > top_k)
        def _():
            need = top_k - count(gt, thr)
            lower = (lax.broadcasted_iota(I32, (t, t), 1)
                     < lax.broadcasted_iota(I32, (t, t), 0))
            lower = jnp.where(lower, 1.0, 0.0).astype(BF16)

            def tie_body(kb, run):
                kk = keys_scr[cur, blk(kb), :]
                eq = jnp.where(kk == thr_row, 1.0, 0.0)
                rank = run + _dot(lower, eq.astype(BF16))
                sel = (kk > thr_row) | ((kk == thr_row) & (rank < need))
                madd_scr[blk(kb), :] = jnp.where(sel, 0.0, NEG)
                return run + jnp.sum(eq, axis=0, keepdims=True)

            lax.fori_loop(0, i + 1, tie_body, jnp.zeros((1, t), F32))

    ones = jnp.ones((SUM_ROWS, t), BF16)

    m_scr[...] = jnp.full(m_scr.shape, M_INIT, F32)
    acc_scr[...] = jnp.zeros(acc_scr.shape, F32)

    def att_block(kb, didx, score_next):
        madd = madd_scr[blk(kb), :]
        for h in range(A_HEADS):
            cols = slice((h // 2) * LANES, (h // 2 + 1) * LANES)
            s = _dot_nt(k_ref[blk(kb), cols], qm_scr[h]) + madd
            if didx is not None:
                s = s + bias_ref[didx, h]
            s_scr[h] = s
            m = m_scr[h]
            m_new = jnp.maximum(m, jnp.max(s, axis=0, keepdims=True))
            alpha_scr[h] = jnp.exp(m - m_new)
            m_scr[h] = m_new

        if score_next:
            score_keys(ikn_ref, iwtn_ref, kb, nxt, False)

        for h in range(A_HEADS):
            pr = jnp.exp(s_scr[h] - m_scr[h]).astype(BF16)
            v1 = jnp.concatenate(
                [vt_ref[h * A_HEAD_DIM:(h + 1) * A_HEAD_DIM, blk(kb)], ones], axis=0)
            acc_scr[h] = alpha_scr[h] * acc_scr[h] + _dot(v1, pr)

    def att_body(kb, c, didx, score_next):
        att_block(kb, didx, score_next)
        return c

    same_seq = i1 > 0
    n_far = jnp.maximum(i - 1, 0)
    for score_next in (True, False):
        on = same_seq if score_next else jnp.logical_not(same_seq)
        lax.fori_loop(0, jnp.where(on, n_far, 0),
                      functools.partial(att_body, didx=None, score_next=score_next), 0)
    for score_next in (True, False):
        on = same_seq if score_next else jnp.logical_not(same_seq)
        lax.fori_loop(n_far, jnp.where(on, i, n_far),
                      functools.partial(att_body, didx=1, score_next=score_next), 0)
    for score_next in (True, False):
        on = same_seq if score_next else jnp.logical_not(same_seq)
        lax.fori_loop(0, jnp.where(on, 1, 0),
                      lambda _, c, sn=score_next: att_body(i, c, 0, sn), 0)
    score_keys(ikn_ref, iwtn_ref, i1, nxt, True)
    for h in range(A_HEADS):
        acc = acc_scr[h]
        ot_scr[h * A_HEAD_DIM:(h + 1) * A_HEAD_DIM, :] = (
            acc[:A_HEAD_DIM] * (1.0 / acc[A_HEAD_DIM:A_HEAD_DIM + 1]))
    o_ref[...] = jnp.transpose(ot_scr[...]).astype(BF16)


def _dsa(q, k, vt, iq, ik2, iwt, bias, bsz, seq):
    t = DSA_TILE
    nq = seq // t
    top_k = min(TOPK_MAX, seq // 4)
    assert t <= top_k and t % CHUNK == 0 and seq % t == 0
    nsteps = bsz * nq
    nxt = lambda s: jnp.minimum(s + 1, nsteps - 1)
    iq_w, ik_w = IDX_HEADS * IDX_DIM, 2 * IDX_DIM
    return pl.pallas_call(
        functools.partial(_dsa_kernel, top_k=top_k, nq=nq, nsteps=nsteps),
        grid=(nsteps,),
        in_specs=[pl.BlockSpec((t, A_WIDTH), lambda s: (s, 0)),
                  pl.BlockSpec((seq, A_WIDTH), lambda s: (s // nq, 0)),
                  pl.BlockSpec((A_WIDTH, seq), lambda s: (s // nq, 0)),
                  pl.BlockSpec((t, iq_w), lambda s: (0, 0)),
                  pl.BlockSpec((seq, ik_w), lambda s: (0, 0)),
                  pl.BlockSpec((IDX_HEADS, t), lambda s: (0, 0)),
                  pl.BlockSpec((t, iq_w), lambda s: (nxt(s), 0)),
                  pl.BlockSpec((seq, ik_w), lambda s: (nxt(s) // nq, 0)),
                  pl.BlockSpec((IDX_HEADS, t), lambda s: (0, nxt(s))),
                  _resident(bias.shape)],
        out_specs=pl.BlockSpec((t, A_WIDTH), lambda s: (s, 0)),
        out_shape=jax.ShapeDtypeStruct((bsz * seq, A_WIDTH), BF16),
        scratch_shapes=[pltpu.VMEM((2, seq, t), F32), pltpu.VMEM((seq, t), F32),
                        pltpu.VMEM((IDX_HEADS, t, LANES), BF16),
                        pltpu.VMEM((A_HEADS, t, LANES), BF16),
                        pltpu.VMEM((A_WIDTH, t), F32),
                        pltpu.VMEM((A_HEADS, 1, t), F32),
                        pltpu.VMEM((A_HEADS, 1, t), F32),
                        pltpu.VMEM((A_HEADS, A_HEAD_DIM + SUM_ROWS, t), F32),
                        pltpu.VMEM((A_HEADS, t, t), F32)],
        compiler_params=_params(("arbitrary",)),
        name="dsa",
    )(q, k, vt, iq, ik2, iwt, iq, ik2, iwt, bias)


HALO = 16


def _ffn_kernel(*refs, n_y, tiles_per_seq, final_norm):
    x_ref, xh_ref = refs[0], refs[1]
    y_refs = refs[2:2 + 2 * n_y]
    (wo_ref, g_ref, wup_ref, cw_ref, cb_ref, wdn_ref, gfin_ref, o_ref,
     ua_scr, ub_scr, act_scr, acc_scr) = refs[2 + 2 * n_y:]
    tm = x_ref.shape[0]

    half = tm // 2
    halves = (slice(0, half), slice(half, tm))

    def mixed(x_r, ys, rows):
        acc = x_r[rows]
        r0 = 0
        for y_r in ys:
            n = y_r.shape[1]
            acc = acc + _dot(y_r[rows], wo_ref[r0:r0 + n, :])
            r0 += n
        return acc

    xh = mixed(xh_ref, y_refs[1::2], slice(0, HALO))
    xs = [mixed(x_ref, y_refs[0::2], r) for r in halves]
    keep = jnp.where(pl.program_id(0) % tiles_per_seq == 0, 0.0, 1.0)
    hh = (_rms(xh, g_ref[...]) * keep).astype(BF16)

    def conv(scr, c0, rows):
        w = cw_ref[:, c0:c0 + FF_COLS]
        r0, r1 = HALO + rows.start, HALO + rows.stop
        return (cb_ref[:, c0:c0 + FF_COLS]
                + w[2:3] * scr[r0:r1]
                + w[1:2] * scr[r0 - 1:r1 - 1]
                + w[0:1] * scr[r0 - 2:r1 - 2])

    def gate(j, rows):
        ca, cb = j * FF_COLS, D_FF + j * FF_COLS
        act_scr[j % 2, rows] = (jax.nn.silu(conv(ua_scr.at[j % nbuf], ca, rows))
                                * conv(ub_scr.at[j % nbuf], cb, rows)).astype(BF16)

    nj = D_FF // FF_COLS
    nbuf = ua_scr.shape[0]

    def up(j, hs, rows=slice(None)):
        ua_scr[j % nbuf, rows] = _dot(hs, wup_ref[:, j * FF_COLS:(j + 1) * FF_COLS])
        ub_scr[j % nbuf, rows] = _dot(hs, wup_ref[:, D_FF + j * FF_COLS:D_FF + (j + 1) * FF_COLS])

    def down(j, rows=slice(None)):
        return _dot(act_scr[j % 2, rows], wdn_ref[j * FF_COLS:(j + 1) * FF_COLS, :])

    hn0 = _rms(xs[0], g_ref[...]).astype(BF16)
    up(0, jnp.concatenate([hh, hn0], axis=0), slice(0, half + HALO))
    hn1 = _rms(xs[1], g_ref[...]).astype(BF16)
    up(0, jnp.concatenate([hn0[half - HALO:], hn1], axis=0), slice(half, tm + HALO))
    he = jnp.concatenate([hh, hn0, hn1], axis=0)
    for j in range(1, nbuf - 1):
        up(j, he)

    for j in range(nj):
        if j + nbuf - 1 < nj:
            up(j + nbuf - 1, he)
        if j == 1:
            acc_scr[...] = down(0)
        elif j > 1:
            acc_scr[...] += down(j - 1)
        if j + 1 < nj:
            gate(j, slice(0, tm))
    for r, x in zip(halves, xs):
        gate(nj - 1, r)
        acc_scr[r] += down(nj - 1, r)
        y = x + acc_scr[r]
        if final_norm:
            y = _rms(y, gfin_ref[...])
        o_ref[r] = y


def _layer_resident(a, layer):
    nd = a.ndim - 1
    return pl.BlockSpec((None,) + a.shape[1:], lambda *_: (layer,) + (0,) * nd,
                        pipeline_mode=pl.Buffered(1))


def _ffn(x2, ys, wo, g, wup, cw, cb, wdn, gfin, seq, layer, final_norm):
    m = x2.shape[0]
    tm = FFN_TILE
    assert seq % tm == 0 and D_FF % FF_COLS == 0 and tm % HALO == 0
    row = lambda n: pl.BlockSpec((tm, n), lambda i: (i, 0))
    halo = lambda n: pl.BlockSpec((HALO, n), lambda i: (jnp.maximum(i * (tm // HALO) - 1, 0), 0))
    y_specs, y_args = [], []
    for y in ys:
        y_specs += [row(y.shape[1]), halo(y.shape[1])]
        y_args += [y, y]
    consts = (wo, g, wup, cw, cb, wdn, gfin)
    const_specs = [_layer_resident(c, layer) if c is wup or c is cw or c is cb or c is wdn
                   else _resident(c.shape) for c in consts]
    return pl.pallas_call(
        functools.partial(_ffn_kernel, n_y=len(ys), tiles_per_seq=seq // tm,
                          final_norm=final_norm),
        grid=(m // tm,),
        in_specs=[row(D_MODEL), halo(D_MODEL)] + y_specs + const_specs,
        out_specs=row(D_MODEL),
        out_shape=jax.ShapeDtypeStruct((m, D_MODEL), F32),
        scratch_shapes=[pltpu.VMEM((FF_AHEAD + 1, tm + HALO, FF_COLS), F32),
                        pltpu.VMEM((FF_AHEAD + 1, tm + HALO, FF_COLS), F32),
                        pltpu.VMEM((2, tm, FF_COLS), BF16),
                        pltpu.VMEM((tm, D_MODEL), F32)],
        compiler_params=_params(("parallel",)),
        name="conv_ffn",
    )(x2, x2, *y_args, *consts)


def _hgrn_kernel(x_ref, g_ref, w_ref, lbp_ref, gout_ref, o_ref, proj_scr, st_scr, *, layer):
    ts = x_ref.shape[0]
    kw = C_HEADS * C_KEY_DIM
    c = CHUNK
    nsub = c // HG_SUB

    @pl.when(pl.program_id(1) == 0)
    def _():
        st_scr[...] = jnp.zeros(st_scr.shape, F32)

    h = _rms(x_ref[...], g_ref[...]).astype(BF16)

    def project(c0, c1):
        pw = 512
        for n in range(c0, c1, pw):
            proj_scr[:, n:n + pw] = _dot(h, w_ref[:, n:n + pw])

    lbp = lbp_ref[...]
    e = jnp.exp(lbp - jnp.max(lbp, axis=0, keepdims=True))
    sm = e / jnp.sum(e, axis=0, keepdims=True)
    lb = jnp.sum(sm[1:layer + 1], axis=0, keepdims=True) if layer > 0 else jnp.zeros((1, kw), F32)

    ri = lax.broadcasted_iota(I32, (c, c), 0)
    ci = lax.broadcasted_iota(I32, (c, c), 1)
    r3 = lax.broadcasted_iota(I32, (c, 3 * c), 0)
    c3 = lax.broadcasted_iota(I32, (c, 3 * c), 1) % c
    tri3 = jnp.where(r3 >= c3, 1.0, 0.0).astype(BF16)
    dmask = (ri >= ci) & (ri // HG_SUB == ci // HG_SUB)

    def gates(rows):
        gg = lb + (1.0 - lb) * jax.nn.sigmoid(proj_scr[rows, kw:2 * kw])
        lg = jnp.log(gg)
        hi = lg.astype(BF16)
        r1 = lg - hi.astype(F32)
        mid = r1.astype(BF16)
        lo = (r1 - mid.astype(F32)).astype(BF16)
        return gg, jnp.concatenate([hi, mid, lo], axis=0)

    def decay_products(rows, gg, b_all):
        out = []
        for hd in range(C_HEADS):
            kc = slice(hd * C_KEY_DIM, (hd + 1) * C_KEY_DIM)
            q = jax.nn.silu(proj_scr[rows, hd * C_KEY_DIM:(hd + 1) * C_KEY_DIM])
            kk = 1.0 - gg[:, kc]
            b = b_all[:, kc]
            qs, ks = [], []
            for j in range(nsub - 1):
                e0, e1 = j * HG_SUB, (j + 1) * HG_SUB
                rj = b[e1 - 1:e1]
                qs.append(jnp.concatenate(
                    [jnp.zeros((e1, C_KEY_DIM), F32), q[e1:] * jnp.exp(b[e1:] - rj)], axis=0))
                kj = kk[e0:e1] * jnp.exp(rj - b[e0:e1])
                parts = [kj, jnp.zeros((c - e1, C_KEY_DIM), F32)]
                if e0:
                    parts = [jnp.zeros((e0, C_KEY_DIM), F32)] + parts
                ks.append(jnp.concatenate(parts, axis=0))
            a_off = _dot_nt(jnp.concatenate(qs, axis=1).astype(BF16),
                            jnp.concatenate(ks, axis=1).astype(BF16))
            bm = jnp.concatenate(
                [jnp.broadcast_to(b[j * HG_SUB + HG_SUB // 2:j * HG_SUB + HG_SUB // 2 + 1],
                                  (HG_SUB, C_KEY_DIM)) for j in range(nsub)], axis=0)
            a_dg = _dot_nt((q * jnp.exp(b - bm)).astype(BF16), (kk * jnp.exp(bm - b)).astype(BF16))
            a = (a_off + jnp.where(dmask, a_dg, 0.0)).astype(BF16)
            bl = b[c - 1:c]
            out.append([(q * jnp.exp(b)).astype(BF16), a,
                        (kk * jnp.exp(bl - b)).astype(BF16), jnp.exp(bl)])
        return out

    def values(rows, hd):
        return proj_scr[rows, 2 * kw + hd * C_VAL_DIM:2 * kw + (hd + 1) * C_VAL_DIM].astype(BF16)

    def back(rows, parts, intras):
        sts = [st_scr[hd] for hd in range(C_HEADS)]
        inters = [_dot_nt(p[0], st.astype(BF16)) for p, st in zip(parts, sts)]
        upds = [_dot_tn(values(rows, hd), p[2]) for hd, p in enumerate(parts)]
        for hd, p in enumerate(parts):
            st_scr[hd] = sts[hd] * p[3] + upds[hd]
            gate = proj_scr[rows, 2 * kw + C_WIDTH + hd * C_VAL_DIM:
                            2 * kw + C_WIDTH + (hd + 1) * C_VAL_DIM]
            o_ref[rows, hd * C_VAL_DIM:(hd + 1) * C_VAL_DIM] = (
                _rms(inters[hd] + intras[hd], gout_ref[...]) * jax.nn.silu(gate)).astype(BF16)

    rows = [slice(n * c, (n + 1) * c) for n in range(ts // c)]
    project(kw, 2 * kw)
    gate_parts = [gates(r) for r in rows]
    project(0, kw)
    cums = [_dot(tri3, lg3) for _, lg3 in gate_parts]
    project(2 * kw, 2 * kw + C_WIDTH)
    parts = [decay_products(r, gg, b_all) for r, (gg, _), b_all in zip(rows, gate_parts, cums)]
    project(2 * kw + C_WIDTH, 2 * kw + 2 * C_WIDTH)
    intras = [[_dot(p[1], values(r, hd)) for hd, p in enumerate(ps)]
              for r, ps in zip(rows, parts)]
    for r, ps, it in zip(rows, parts, intras):
        back(r, ps, it)


def _hgrn(x2, g, w, lbp, gout, bsz, seq, layer):
    ts = HG_TILE
    nt = seq // ts
    assert seq % ts == 0 and ts % CHUNK == 0 and CHUNK % HG_SUB == 0
    row = lambda n: pl.BlockSpec((ts, n), lambda b, i: (b * nt + i, 0))
    return pl.pallas_call(
        functools.partial(_hgrn_kernel, layer=layer),
        grid=(bsz, nt),
        in_specs=[row(D_MODEL), _resident(g.shape), _resident(w.shape), _resident(lbp.shape),
                  _resident(gout.shape)],
        out_specs=row(C_WIDTH),
        out_shape=jax.ShapeDtypeStruct((bsz * seq, C_WIDTH), BF16),
        scratch_shapes=[pltpu.VMEM((ts, w.shape[1]), F32),
                        pltpu.VMEM((C_HEADS, C_VAL_DIM, C_KEY_DIM), F32)],
        compiler_params=_params(("parallel", "arbitrary")),
        name="hgrn2",
    )(x2, g, w, lbp, gout)


def kernel(x, rel_bias, hgrn_lb, mix_norm, ffn_norm, final_norm, ab_w_in, ab_idx_k_norm, ab_gmlp_norm, ab_w_s, ab_b_s, ab_w_out, c_w_in, c_out_norm, c_w_out, ffn_w_up, ffn_conv_w, ffn_conv_b, ffn_w_down):
    bsz, seq, d = x.shape
    depth = mix_norm.shape[0]
    x2 = x.reshape(bsz * seq, d)
    row = lambda a: a.reshape(1, -1)
    bias = _bias_tiles(rel_bias)
    wup_all, wdn_all = ffn_w_up.astype(BF16), ffn_w_down.astype(BF16)
    cb_all = ffn_conv_b.reshape(depth, 1, -1)
    for l in range(depth):
        if l % 2 == 0:
            e = l // 2
            w = ab_w_in[e]
            offs = [0]
            for s in (A_WIDTH, A_WIDTH, A_WIDTH, IDX_HEADS * IDX_DIM, IDX_DIM, IDX_HEADS,
                      B_WIDTH, B_WIDTH):
                offs.append(offs[-1] + s)
            seg = [w[:, offs[n]:offs[n + 1]] for n in range(8)]
            assert offs[4] == _C_END
            w_head = w[:, :_C_END].astype(BF16)
            w_tail = jnp.concatenate(
                [seg[6], seg[7], seg[4], seg[4], seg[5],
                 jnp.zeros((d, LANES - IDX_HEADS), w.dtype)], axis=1).astype(BF16)
            assert w_tail.shape[1] == _T_END
            gik = jnp.concatenate([ab_idx_k_norm[e], ab_idx_k_norm[e]])
            q, k, vt, iq, ik2, iwt, y_b = _proj_ab(
                x2, row(mix_norm[l]), w_head, w_tail, row(gik), row(ab_gmlp_norm[e]),
                ab_w_s[e], jnp.transpose(ab_b_s[e]), seq)
            y_a = _dsa(q, k, vt, iq, ik2, iwt, bias, bsz, seq)
            ys, wo = [y_a, y_b], ab_w_out[e]
        else:
            o_i = l // 2
            og = _hgrn(x2, row(mix_norm[l]), c_w_in[o_i].astype(BF16), hgrn_lb,
                       row(c_out_norm[o_i]), bsz, seq, l)
            ys, wo = [og], c_w_out[o_i]
        x2 = _ffn(x2, ys, wo.astype(BF16), row(ffn_norm[l]), wup_all, ffn_conv_w, cb_all,
                  wdn_all, row(final_norm), seq, l, l == depth - 1)
    return x2.reshape(bsz, seq, d)
```
